```python
import math
import jax
import jax.numpy as jnp
from jax import lax
import numpy as np

D_MODEL = 2048
BATCH = 32
SEQ = 256
DEPTH = 4
DEC_BATCH = 2
DEC_SEQ = 4096
PAST_LEN = 256

GRID_W = 64
N_DIFF_HEADS = 8
DIFF_HEAD_DIM = 64
DIFF_V_DIM = 2 * DIFF_HEAD_DIM
ROPE_AXIS_DIM = DIFF_HEAD_DIM // 2
ROPE_BASE = 10000.0
Q_BLOCK = 128
N_GLA_HEADS = 4
GLA_DK = D_MODEL // 16
GLA_DV = D_MODEL // 8
GLA_GATE_RANK = 16
GLA_GATE_TAU = 16.0
GLA_CHUNK = 64
N_EXPERTS = 32
TOP_K = 4
D_FF = D_MODEL
SWIGLU_LIMIT = 7.0
SWIGLU_ALPHA = 1.702
MOE_BLOCK = 256
EPS = 1e-6

DIFF_QK_W = N_DIFF_HEADS * 2 * DIFF_HEAD_DIM
DIFF_VW = N_DIFF_HEADS * DIFF_V_DIM
GLA_KW = N_GLA_HEADS * GLA_DK
GLA_VW = N_GLA_HEADS * GLA_DV
MIX_W = DIFF_VW + GLA_VW
IN_DIM = 2 * DIFF_QK_W + DIFF_VW + 2 * GLA_KW + 2 * GLA_VW + 2 * GLA_GATE_RANK
IN_SPLITS = (DIFF_QK_W,
             2 * DIFF_QK_W,
             2 * DIFF_QK_W + DIFF_VW,
             2 * DIFF_QK_W + DIFF_VW + GLA_KW,
             2 * DIFF_QK_W + DIFF_VW + 2 * GLA_KW,
             2 * DIFF_QK_W + DIFF_VW + 2 * GLA_KW + GLA_VW,
             2 * DIFF_QK_W + DIFF_VW + 2 * GLA_KW + 2 * GLA_VW)

kernel_name = 'hybrid_gla_diffattn_moe_dit_step'


def rmsnorm(x, w):
    xf = x.astype(jnp.float32)
    y = xf * lax.rsqrt(jnp.mean(xf * xf, axis=-1, keepdims=True) + EPS)
    return (y * w.astype(jnp.float32)).astype(x.dtype)


def adaln(cond, w, b):
    return jax.nn.silu(cond) @ w + b


def modulate(x, shift, scale):
    return x * (1 + scale) + shift


def axial_rope_tables(n_tok):
    rows = n_tok // GRID_W
    r = jnp.repeat(jnp.arange(rows), GRID_W).astype(jnp.float32)
    col = jnp.tile(jnp.arange(GRID_W), rows).astype(jnp.float32)
    inv = ROPE_BASE ** (-jnp.arange(0, ROPE_AXIS_DIM, 2, dtype=jnp.float32) / ROPE_AXIS_DIM)
    ar = r[:, None] * inv
    ac = col[:, None] * inv
    return jnp.cos(ar), jnp.sin(ar), jnp.cos(ac), jnp.sin(ac)


def _rotate(x, cos, sin):
    x1, x2 = jnp.split(x, 2, axis=-1)
    cos = cos[None, :, None, None, :]
    sin = sin[None, :, None, None, :]
    return jnp.concatenate([x1 * cos - x2 * sin, x2 * cos + x1 * sin], axis=-1)


def apply_axial_rope(x, tabs):
    cr, sr, cc, sc = tabs
    xf = x.astype(jnp.float32)
    y = jnp.concatenate([_rotate(xf[..., :ROPE_AXIS_DIM], cr, sr),
                         _rotate(xf[..., ROPE_AXIS_DIM:], cc, sc)], axis=-1)
    return y.astype(x.dtype)


def diff_lambda(lw, lam_init):
    lw = lw.astype(jnp.float32)
    return jnp.exp(jnp.sum(lw[0] * lw[1])) - jnp.exp(jnp.sum(lw[2] * lw[3])) + lam_init


def diff_attend(q, k, v, lam):
    s = jnp.einsum('bqhmd,bkhmd->bhmqk', q, k,
                   preferred_element_type=jnp.float32) * (DIFF_HEAD_DIM ** -0.5)
    p = jax.nn.softmax(s, axis=-1)
    a = p[:, :, 0] - lam * p[:, :, 1]
    return jnp.einsum('bhqk,bkhv->bqhv', a.astype(v.dtype), v)


def diff_attend_blocked(q, k, v, lam):
    B, T = q.shape[:2]
    nb = T // Q_BLOCK
    qb = jnp.moveaxis(q.reshape(B, nb, Q_BLOCK, N_DIFF_HEADS, 2, DIFF_HEAD_DIM), 1, 0)
    ob = lax.map(lambda qq: diff_attend(qq, k, v, lam), qb)
    return jnp.moveaxis(ob, 0, 1).reshape(B, T, N_DIFF_HEADS, DIFF_V_DIM)


def gla_scan(q, k, v, log_a, s0):
    B, T, H, _ = q.shape
    Dv = v.shape[-1]
    n = T // GLA_CHUNK
    q, k, v, log_a = (t.astype(jnp.float32).reshape(B, n, GLA_CHUNK, H, t.shape[-1])
                      for t in (q, k, v, log_a))
    b = jnp.cumsum(log_a, axis=2)
    b_last = b[:, :, -1]
    q_dec = q * jnp.exp(b)
    k_inv = k * jnp.exp(-b)
    k_end = k * jnp.exp(b_last[:, :, None] - b)
    lower = jnp.tril(jnp.ones((GLA_CHUNK, GLA_CHUNK), jnp.float32))
    att = jnp.einsum('bnihd,bnjhd->bnhij', q_dec, k_inv) * lower
    o_intra = jnp.einsum('bnhij,bnjhv->bnihv', att, v)
    u = jnp.einsum('bnjhd,bnjhv->bnhdv', k_end, v)
    decay = jnp.exp(b_last)

    def step(s, inp):
        dec, uc = inp
        return dec[..., None] * s + uc, s

    s_final, s_prev = lax.scan(step, s0.astype(jnp.float32),
                               (jnp.moveaxis(decay, 1, 0), jnp.moveaxis(u, 1, 0)))
    o_inter = jnp.einsum('bnihd,nbhdv->bnihv', q_dec, s_prev)
    return (o_intra + o_inter).reshape(B, T, H, Dv), s_final


def mix_tokens(h, lp, lam_init, ctx_k, ctx_v, s0_f, s0_b, rope):
    B, T, _ = h.shape
    dq, dk, dv, gq, gk, gv, gg, ga = jnp.split(h @ lp['w_in'], IN_SPLITS, axis=-1)
    dq = rmsnorm(dq.reshape(B, T, N_DIFF_HEADS, 2, DIFF_HEAD_DIM), lp['q_norm'])
    dk = rmsnorm(dk.reshape(B, T, N_DIFF_HEADS, 2, DIFF_HEAD_DIM), lp['k_norm'])
    dv = dv.reshape(B, T, N_DIFF_HEADS, DIFF_V_DIM)
    lam = diff_lambda(lp['lam'], lam_init)
    if rope is None:
        o_diff = diff_attend(dq, dk, dv, lam)
    else:
        k_all = jnp.concatenate([apply_axial_rope(dk, rope), ctx_k.astype(dk.dtype)], axis=1)
        v_all = jnp.concatenate([dv, ctx_v.astype(dv.dtype)], axis=1)
        o_diff = diff_attend_blocked(apply_axial_rope(dq, rope), k_all, v_all, lam)
    o_diff = rmsnorm(o_diff, lp['diff_out_norm']) * (1.0 - lam_init)
    la = jax.nn.log_sigmoid(
        jnp.einsum('btmr,mrk->btmk', ga.reshape(B, T, 2, GLA_GATE_RANK).astype(jnp.float32),
                   lp['w_gla_a2'].astype(jnp.float32)) + lp['b_gla_a'].astype(jnp.float32)) / GLA_GATE_TAU
    la = la.reshape(B, T, 2, N_GLA_HEADS, GLA_DK)
    gq = gq.reshape(B, T, N_GLA_HEADS, GLA_DK) * (GLA_DK ** -0.5)
    gk = gk.reshape(B, T, N_GLA_HEADS, GLA_DK)
    gv = gv.reshape(B, T, N_GLA_HEADS, GLA_DV)
    o_f, s_f = gla_scan(gq, gk, gv, la[:, :, 0], s0_f)
    o_b, s_b = gla_scan(jnp.flip(gq, 1), jnp.flip(gk, 1), jnp.flip(gv, 1),
                        jnp.flip(la[:, :, 1], 1), s0_b)
    o_gla = rmsnorm(o_f + jnp.flip(o_b, 1), lp['gla_out_norm']).astype(h.dtype)
    o_gla = o_gla * jax.nn.silu(gg).reshape(B, T, N_GLA_HEADS, GLA_DV)
    mixed = jnp.concatenate([o_diff.reshape(B, T, DIFF_VW), o_gla.reshape(B, T, GLA_VW)], axis=-1)
    return mixed @ lp['w_out'], dk, dv, s_f, s_b


def moe_ffn(h, lp):
    B, T, D = h.shape
    x = h.reshape(B * T, D)
    n_tok = B * T
    n_items = n_tok * TOP_K
    logits = (x @ lp['router_w'] + lp['router_b']).astype(jnp.float32)
    top_val, top_idx = lax.top_k(logits, TOP_K)
    gates = jax.nn.softmax(top_val, axis=-1)
    flat_e = top_idx.reshape(-1)
    flat_tok = jnp.arange(n_items, dtype=jnp.int32) // TOP_K
    order = jnp.argsort(flat_e)
    sorted_e = flat_e[order]
    sorted_tok = flat_tok[order]
    counts = jnp.bincount(flat_e, length=N_EXPERTS)
    starts = jnp.cumsum(counts) - counts
    padded = (counts + MOE_BLOCK - 1) // MOE_BLOCK * MOE_BLOCK
    pad_end = jnp.cumsum(padded)
    pad_start = pad_end - padded
    dest = pad_start[sorted_e] + jnp.arange(n_items, dtype=jnp.int32) - starts[sorted_e]
    n_blocks = -(-n_items // MOE_BLOCK) + N_EXPERTS
    buf_tok = jnp.zeros((n_blocks * MOE_BLOCK,), jnp.int32).at[dest].set(sorted_tok)
    block_e = jnp.minimum(
        jnp.searchsorted(pad_end, jnp.arange(n_blocks, dtype=pad_end.dtype) * MOE_BLOCK, side='right'),
        N_EXPERTS - 1)
    xb = x[buf_tok].reshape(n_blocks, MOE_BLOCK, D)
    w_gu, b_gu, w_dn, b_dn = lp['w_gate_up'], lp['b_gate_up'], lp['w_down'], lp['b_down']

    def expert_block(args):
        xe, e = args
        gu = xe @ w_gu[e] + b_gu[e]
        x_glu, x_lin = jnp.split(gu, 2, axis=-1)
        x_glu = jnp.minimum(x_glu, SWIGLU_LIMIT)
        x_lin = jnp.clip(x_lin, -SWIGLU_LIMIT, SWIGLU_LIMIT)
        act = x_glu * jax.nn.sigmoid(SWIGLU_ALPHA * x_glu) * (x_lin + 1)
        return act @ w_dn[e] + b_dn[e]

    yb = lax.map(expert_block, (xb, block_e)).reshape(n_blocks * MOE_BLOCK, D)
    item_dest = jnp.zeros((n_items,), jnp.int32).at[order].set(dest)
    y = yb[item_dest].reshape(n_tok, TOP_K, D)
    return jnp.einsum('tkd,tk->td', y, gates.astype(y.dtype)).reshape(B, T, D)


def trunk_layer(x, mod, lp, lam_init, ctx_k, ctx_v, s0_f, s0_b, rope):
    sh_a, sc_a, g_a, sh_f, sc_f, g_f = jnp.split(mod[:, None, :], 6, axis=-1)
    h = modulate(rmsnorm(x, lp['norm_attn']), sh_a, sc_a)
    mixed, k, v, s_f, s_b = mix_tokens(h, lp, lam_init, ctx_k, ctx_v, s0_f, s0_b, rope)
    x = x + g_a * mixed
    h = modulate(rmsnorm(x, lp['norm_ffn']), sh_f, sc_f)
    x = x + g_f * moe_ffn(h, lp)
    return x, k, v, s_f, s_b


def setup_inputs(seed: int = 0) -> dict:
    key = jax.random.key(seed)
    ks = jax.random.split(key, 26)
    D = D_MODEL

    def nrm(k, shape, scale):
        return jax.random.normal(k, shape, jnp.float32) * scale

    return {
        'x_prompt': nrm(ks[0], (BATCH, SEQ, D), 1.0),
        'x_sample': nrm(ks[1], (DEC_BATCH, DEC_SEQ, D), 1.0),
        'cache_diff_k': nrm(ks[2], (DEC_BATCH, DEPTH, PAST_LEN, N_DIFF_HEADS, 2, DIFF_HEAD_DIM), 1.0),
        'cache_diff_v': nrm(ks[3], (DEC_BATCH, DEPTH, PAST_LEN, N_DIFF_HEADS, DIFF_V_DIM), 1.0),
        'state_gla': nrm(ks[4], (DEC_BATCH, DEPTH, 2, N_GLA_HEADS, GLA_DK, GLA_DV), 1.0),
        'c': nrm(ks[5], (DEC_BATCH, D), 1.0),
        'c_ctx': nrm(ks[6], (D,), 1.0),
        'norm_attn_w': 1.0 + nrm(ks[7], (DEPTH, D), 0.1),
        'norm_ffn_w': 1.0 + nrm(ks[8], (DEPTH, D), 0.1),
        'w_ada': nrm(ks[9], (DEPTH, D, 6 * D), 0.5 * D ** -0.5),
        'b_ada': nrm(ks[10], (DEPTH, 6 * D), 0.02),
        'w_in': nrm(ks[11], (DEPTH, D, IN_DIM), D ** -0.5),
        'q_norm_w': 1.0 + nrm(ks[12], (DEPTH, DIFF_HEAD_DIM), 0.1),
        'k_norm_w': 1.0 + nrm(ks[13], (DEPTH, DIFF_HEAD_DIM), 0.1),
        'diff_lambda_w': nrm(ks[14], (DEPTH, 4, DIFF_HEAD_DIM), 0.1),
        'diff_out_norm_w': 1.0 + nrm(ks[15], (DEPTH, DIFF_V_DIM), 0.1),
        'w_gla_a2': nrm(ks[16], (DEPTH, 2, GLA_GATE_RANK, GLA_KW), GLA_GATE_RANK ** -0.5),
        'b_gla_a': nrm(ks[17], (DEPTH, 2, GLA_KW), 0.1),
        'gla_out_norm_w': 1.0 + nrm(ks[18], (DEPTH, GLA_DV), 0.1),
        'w_out': nrm(ks[19], (DEPTH, MIX_W, D), MIX_W ** -0.5),
        'router_w': nrm(ks[20], (DEPTH, D, N_EXPERTS), D ** -0.5),
        'router_b': nrm(ks[21], (DEPTH, N_EXPERTS), 0.01),
        'w_gate_up': nrm(ks[22], (DEPTH, N_EXPERTS, D, 2 * D_FF), D ** -0.5),
        'b_gate_up': nrm(ks[23], (DEPTH, N_EXPERTS, 2 * D_FF), 0.01),
        'w_down': nrm(ks[24], (DEPTH, N_EXPERTS, D_FF, D), D_FF ** -0.5),
        'b_down': nrm(ks[25], (DEPTH, N_EXPERTS, D), 0.01),
    }


def reference(x_prompt, x_sample, cache_diff_k, cache_diff_v, state_gla, c, c_ctx,
              norm_attn_w, norm_ffn_w, w_ada, b_ada, w_in, q_norm_w, k_norm_w, diff_lambda_w,
              diff_out_norm_w, w_gla_a2, b_gla_a, gla_out_norm_w, w_out, router_w, router_b,
              w_gate_up, b_gate_up, w_down, b_down):
    B = x_prompt.shape[0]
    n_lat = x_sample.shape[1]
    rope = axial_rope_tables(n_lat)
    zero_state = jnp.zeros((B, N_GLA_HEADS, GLA_DK, GLA_DV), jnp.float32)
    xp, xs = x_prompt, x_sample
    ks, vs, sts = [], [], []
    for l in range(DEPTH):
        lam_init = 0.8 - 0.6 * math.exp(-0.3 * l)
        lp = {
            'norm_attn': norm_attn_w[l], 'norm_ffn': norm_ffn_w[l], 'w_in': w_in[l],
            'q_norm': q_norm_w[l], 'k_norm': k_norm_w[l], 'lam': diff_lambda_w[l],
            'diff_out_norm': diff_out_norm_w[l], 'w_gla_a2': w_gla_a2[l], 'b_gla_a': b_gla_a[l],
            'gla_out_norm': gla_out_norm_w[l], 'w_out': w_out[l],
            'router_w': router_w[l], 'router_b': router_b[l],
            'w_gate_up': w_gate_up[l], 'b_gate_up': b_gate_up[l],
            'w_down': w_down[l], 'b_down': b_down[l],
        }
        mod_ctx = adaln(c_ctx[None, :], w_ada[l], b_ada[l])
        mod_lat = adaln(c, w_ada[l], b_ada[l])
        xp, k_l, v_l, s_f, s_b = trunk_layer(xp, mod_ctx, lp, lam_init, None, None,
                                             zero_state, zero_state, None)
        ks.append(k_l)
        vs.append(v_l)
        sts.append(jnp.stack([s_f, s_b], axis=1).astype(x_prompt.dtype))
        xs, _, _, _, _ = trunk_layer(xs, mod_lat, lp, lam_init, cache_diff_k[:, l], cache_diff_v[:, l],
                                     state_gla[:, l, 0], state_gla[:, l, 1], rope)
    new_cache_diff_k = jnp.stack(ks, axis=1)
    new_cache_diff_v = jnp.stack(vs, axis=1)
    new_state_gla = jnp.stack(sts, axis=1)
    return (xp, xs, new_cache_diff_k, new_cache_diff_v, new_state_gla)
```

```python
import functools
import math

import jax
import jax.numpy as jnp
from jax import lax
from jax.experimental import pallas as pl
from jax.experimental.pallas import tpu as pltpu

F32 = jnp.float32
BF16 = jnp.bfloat16

D_MODEL = 2048
GRID_W = 64
N_DIFF_HEADS = 8
DIFF_HEAD_DIM = 64
HEAD_W = 2 * DIFF_HEAD_DIM
ROPE_AXIS_DIM = DIFF_HEAD_DIM // 2
ROPE_BASE = 10000.0
N_GLA_HEADS = 4
GLA_DK = 128
GLA_DV = 256
GLA_GATE_RANK = 16
GLA_GATE_TAU = 16.0
GLA_CHUNK = 64
N_EXPERTS = 32
TOP_K = 4
D_FF = D_MODEL
SWIGLU_LIMIT = 7.0
SWIGLU_ALPHA = 1.702
EPS = 1e-6

DIFF_W = N_DIFF_HEADS * HEAD_W
GLA_KW = N_GLA_HEADS * GLA_DK
GLA_VW = N_GLA_HEADS * GLA_DV
PROJ_W = 3 * DIFF_W + 2 * GLA_KW + 2 * GLA_VW
LANES = 128
ROW_TILE = 256
MOE_ROWS = 512
NEG_BIG = -1e30


def _cparams(sem, vmem_mb):
    return pltpu.CompilerParams(dimension_semantics=sem, vmem_limit_bytes=vmem_mb * 2**20)


def _dot(a, b):
    return jnp.dot(a, b, preferred_element_type=F32)


def _split_dot(a, b, dims=None):
    hi = a.astype(BF16)
    lo = (a - hi.astype(F32)).astype(BF16)
    if dims is None:
        return _dot(hi, b) + _dot(lo, b)
    return (lax.dot_general(hi, b, dims, preferred_element_type=F32)
            + lax.dot_general(lo, b, dims, preferred_element_type=F32))


def _adaln_kernel(c_ref, w_ref, b_ref, o_ref):
    c = c_ref[...]
    s = c * (1.0 / (1.0 + jnp.exp(-c)))
    o_ref[...] = _dot(s.astype(BF16), w_ref[...].astype(BF16)) + b_ref[...]


def _adaln(cond8, w_ada, b_ada):
    depth, d, n = w_ada.shape
    tn = 1024
    return pl.pallas_call(
        _adaln_kernel,
        grid=(depth, n // tn),
        in_specs=[pl.BlockSpec((8, d), lambda l, j: (0, 0)),
                  pl.BlockSpec((None, d, tn), lambda l, j: (l, 0, j)),
                  pl.BlockSpec((None, 1, tn), lambda l, j: (l, 0, j))],
        out_specs=pl.BlockSpec((None, 8, tn), lambda l, j: (l, 0, j)),
        out_shape=jax.ShapeDtypeStruct((depth, 8, n), F32),
        compiler_params=_cparams(("parallel", "parallel"), 40),
        name="adaln",
    )(cond8, w_ada, b_ada.reshape(depth, 1, n))


def _mod_index(l, chunk, tm, n_ctx, dec_seq):
    def index_map(i, *_):
        r = jnp.where(i * tm < n_ctx, 0, 1 + (i * tm - n_ctx) // dec_seq)
        return ((l * 8 + r) * 6 + chunk, 0, 0)
    return index_map


def _in_proj_kernel(x_ref, nw_ref, sh_ref, sc_ref, w_ref, wga_ref, a2_ref, ba_ref,
                    o_ref, la_ref, h_scr):
    @pl.when(pl.program_id(1) == 0)
    def _():
        x = x_ref[...]
        y = x * lax.rsqrt(jnp.mean(x * x, axis=-1, keepdims=True) + EPS) * nw_ref[...]
        hb = (y * (1.0 + sc_ref[...]) + sh_ref[...]).astype(BF16)
        h_scr[...] = hb
        for m in range(2):
            ga = _dot(hb, wga_ref[m])
            z = _dot(ga.astype(BF16), a2_ref[m].astype(BF16)) + ba_ref[m]
            ls = jnp.minimum(z, 0.0) - jnp.log1p(jnp.exp(-jnp.abs(z)))
            la_ref[:, m * GLA_KW:(m + 1) * GLA_KW] = ls * (1.0 / GLA_GATE_TAU)

    o_ref[...] = _dot(h_scr[...], w_ref[...])


def _in_proj(x_all, norm_w, mod_tab, w_main, w_ga, a2_pad, b_a, l, n_ctx, dec_seq, tm):
    n, d = x_all.shape
    tn = 1024
    return pl.pallas_call(
        _in_proj_kernel,
        grid=(n // tm, PROJ_W // tn),
        in_specs=[pl.BlockSpec((tm, d), lambda i, j: (i, 0)),
                  pl.BlockSpec((None, 1, d), lambda i, j: (l, 0, 0)),
                  pl.BlockSpec((None, 1, d), _mod_index(l, 0, tm, n_ctx, dec_seq)),
                  pl.BlockSpec((None, 1, d), _mod_index(l, 1, tm, n_ctx, dec_seq)),
                  pl.BlockSpec((None, d, tn), lambda i, j: (l, 0, j)),
                  pl.BlockSpec((None, 2, d, LANES), lambda i, j: (l, 0, 0, 0)),
                  pl.BlockSpec((None, 2, LANES, GLA_KW), lambda i, j: (l, 0, 0, 0)),
                  pl.BlockSpec((None, 2, 1, GLA_KW), lambda i, j: (l, 0, 0, 0))],
        out_specs=[pl.BlockSpec((tm, tn), lambda i, j: (i, j)),
                   pl.BlockSpec((tm, 2 * GLA_KW), lambda i, j: (i, 0))],
        out_shape=[jax.ShapeDtypeStruct((n, PROJ_W), F32),
                   jax.ShapeDtypeStruct((n, 2 * GLA_KW), F32)],
        scratch_shapes=[pltpu.VMEM((tm, d), BF16)],
        compiler_params=_cparams(("parallel", "arbitrary"), 48),
        name="in_proj",
    )(x_all, norm_w, mod_tab, mod_tab, w_main, w_ga, a2_pad, b_a)


def _group_ones():
    r = lax.broadcasted_iota(jnp.int32, (HEAD_W, HEAD_W), 0) // DIFF_HEAD_DIM
    c = lax.broadcasted_iota(jnp.int32, (HEAD_W, HEAD_W), 1) // DIFF_HEAD_DIM
    return (r == c).astype(BF16)


def _head_norm(x, w, g):
    ss = _split_dot(x * x, g)
    return x * lax.rsqrt(ss * (1.0 / DIFF_HEAD_DIM) + EPS) * w


def _rope(y, cos, sin_signed, lo_mask):
    half = ROPE_AXIS_DIM // 2
    partner = jnp.where(lo_mask, pltpu.roll(y, HEAD_W - half, 1), pltpu.roll(y, half, 1))
    return y * cos + partner * sin_signed


def _prep_ctx_kernel(q_ref, k_ref, v_ref, qw_ref, kw_ref, qh_ref, kt_ref, vh_ref, kc_ref):
    g = _group_ones()
    for h in range(N_DIFF_HEADS):
        sl = slice(h * HEAD_W, (h + 1) * HEAD_W)
        qn = _head_norm(q_ref[:, sl], qw_ref[...], g)
        kn = _head_norm(k_ref[:, sl], kw_ref[...], g)
        qh_ref[:, sl] = (qn * DIFF_HEAD_DIM ** -0.5).astype(BF16)
        kc_ref[:, sl] = kn
        kt_ref[h] = kn.T.astype(BF16)
    vh_ref[...] = v_ref[...].astype(BF16)


def _prep_ctx(proj, qw, kw, batch, seq):
    tm = ROW_TILE
    nt = seq // tm
    row = lambda c: (lambda b, j: (b * nt + j, c))
    return pl.pallas_call(
        _prep_ctx_kernel,
        grid=(batch, nt),
        in_specs=[pl.BlockSpec((tm, DIFF_W), row(0)),
                  pl.BlockSpec((tm, DIFF_W), row(1)),
                  pl.BlockSpec((tm, DIFF_W), row(2)),
                  pl.BlockSpec((1, HEAD_W), lambda b, j: (0, 0)),
                  pl.BlockSpec((1, HEAD_W), lambda b, j: (0, 0))],
        out_specs=[pl.BlockSpec((None, tm, DIFF_W), lambda b, j: (b, j, 0)),
                   pl.BlockSpec((None, N_DIFF_HEADS, HEAD_W, tm), lambda b, j: (b, 0, 0, j)),
                   pl.BlockSpec((None, tm, DIFF_W), lambda b, j: (b, j, 0)),
                   pl.BlockSpec((None, tm, DIFF_W), lambda b, j: (b, j, 0))],
        out_shape=[jax.ShapeDtypeStruct((batch, seq, DIFF_W), BF16),
                   jax.ShapeDtypeStruct((batch, N_DIFF_HEADS, HEAD_W, seq), BF16),
                   jax.ShapeDtypeStruct((batch, seq, DIFF_W), BF16),
                   jax.ShapeDtypeStruct((batch, seq, DIFF_W), F32)],
        compiler_params=_cparams(("parallel", "parallel"), 40),
        name="prep_ctx",
    )(proj, proj, proj, qw, kw)


def _prep_lat_kernel(q_ref, k_ref, v_ref, ck_ref, cv_ref, qw_ref, kw_ref, cos_ref, sin_ref,
                     qh_ref, kt_ref, vh_ref, *, n_lat_tiles):
    j = pl.program_id(1)

    @pl.when(j < n_lat_tiles)
    def _():
        g = _group_ones()
        cos = cos_ref[...]
        sin = sin_ref[...]
        lane = lax.broadcasted_iota(jnp.int32, cos.shape, 1)
        lo_mask = (lane % ROPE_AXIS_DIM) < (ROPE_AXIS_DIM // 2)
        for h in range(N_DIFF_HEADS):
            sl = slice(h * HEAD_W, (h + 1) * HEAD_W)
            qn = _rope(_head_norm(q_ref[:, sl], qw_ref[...], g), cos, sin, lo_mask)
            kn = _rope(_head_norm(k_ref[:, sl], kw_ref[...], g), cos, sin, lo_mask)
            qh_ref[:, sl] = (qn * DIFF_HEAD_DIM ** -0.5).astype(BF16)
            kt_ref[h] = kn.T.astype(BF16)
        vh_ref[...] = v_ref[...].astype(BF16)

    @pl.when(j >= n_lat_tiles)
    def _():
        for h in range(N_DIFF_HEADS):
            kt_ref[h] = ck_ref[:, h * HEAD_W:(h + 1) * HEAD_W].T.astype(BF16)
        vh_ref[...] = cv_ref[...].astype(BF16)


def _prep_lat(proj, cache_k, cache_v, qw, kw, cos_t, sin_t, l, n_ctx, dec_batch, dec_seq, past):
    tm = ROW_TILE
    nl = dec_seq // tm
    npast = past // tm
    row0 = n_ctx // tm
    row = lambda c: (lambda b, j: (row0 + b * nl + jnp.minimum(j, nl - 1), c))
    cache = lambda b, j: (b, l, jnp.maximum(j - nl, 0), 0)
    tab = lambda b, j: (jnp.minimum(j, nl - 1), 0)
    return pl.pallas_call(
        functools.partial(_prep_lat_kernel, n_lat_tiles=nl),
        grid=(dec_batch, nl + npast),
        in_specs=[pl.BlockSpec((tm, DIFF_W), row(0)),
                  pl.BlockSpec((tm, DIFF_W), row(1)),
                  pl.BlockSpec((tm, DIFF_W), row(2)),
                  pl.BlockSpec((None, None, tm, DIFF_W), cache),
                  pl.BlockSpec((None, None, tm, DIFF_W), cache),
                  pl.BlockSpec((1, HEAD_W), lambda b, j: (0, 0)),
                  pl.BlockSpec((1, HEAD_W), lambda b, j: (0, 0)),
                  pl.BlockSpec((tm, HEAD_W), tab),
                  pl.BlockSpec((tm, HEAD_W), tab)],
        out_specs=[pl.BlockSpec((None, tm, DIFF_W), lambda b, j: (b, jnp.minimum(j, nl - 1), 0)),
                   pl.BlockSpec((None, N_DIFF_HEADS, HEAD_W, tm), lambda b, j: (b, 0, 0, j)),
                   pl.BlockSpec((None, tm, DIFF_W), lambda b, j: (b, j, 0))],
        out_shape=[jax.ShapeDtypeStruct((dec_batch, dec_seq, DIFF_W), BF16),
                   jax.ShapeDtypeStruct((dec_batch, N_DIFF_HEADS, HEAD_W, dec_seq + past), BF16),
                   jax.ShapeDtypeStruct((dec_batch, dec_seq + past, DIFF_W), BF16)],
        compiler_params=_cparams(("parallel", "arbitrary"), 40),
        name="prep_lat",
    )(proj, proj, proj, cache_k, cache_v, qw, kw, cos_t, sin_t)


def _rope_tables(n_tok):
    t = jnp.arange(n_tok)
    r = (t // GRID_W).astype(F32)
    c = (t % GRID_W).astype(F32)
    inv = ROPE_BASE ** (-jnp.arange(0, ROPE_AXIS_DIM, 2, dtype=F32) / ROPE_AXIS_DIM)
    ar = r[:, None] * inv
    ac = c[:, None] * inv
    cos64 = jnp.concatenate([jnp.cos(ar), jnp.cos(ar), jnp.cos(ac), jnp.cos(ac)], axis=-1)
    sin64 = jnp.concatenate([-jnp.sin(ar), jnp.sin(ar), -jnp.sin(ac), jnp.sin(ac)], axis=-1)
    return jnp.tile(cos64, (1, 2)), jnp.tile(sin64, (1, 2))


def _attn_kernel(q_ref, kt_ref, v_ref, lw_ref, ow_ref, o_ref, *, lam_init):
    q = q_ref[...]
    lane = lax.broadcasted_iota(jnp.int32, q.shape, 1)
    kt = kt_ref[...]
    v = v_ref[...]

    def attend(qm):
        s = _dot(qm, kt)
        p = jnp.exp(s - jnp.max(s, axis=-1, keepdims=True))
        denom = jnp.sum(p, axis=-1, keepdims=True)
        return _dot(p.astype(BF16), v) / denom

    o1 = attend(jnp.where(lane < DIFF_HEAD_DIM, q, jnp.zeros_like(q)))
    o2 = attend(jnp.where(lane >= DIFF_HEAD_DIM, q, jnp.zeros_like(q)))
    lw = lw_ref[...]
    lam = (jnp.exp(jnp.sum(lw[0:1] * lw[1:2], axis=-1, keepdims=True))
           - jnp.exp(jnp.sum(lw[2:3] * lw[3:4], axis=-1, keepdims=True)) + lam_init)
    o = o1 - lam * o2
    y = o * lax.rsqrt(jnp.mean(o * o, axis=-1, keepdims=True) + EPS) * ow_ref[...]
    o_ref[...] = (y * (1.0 - lam_init)).astype(BF16)


def _attention(qh, kt, vh, lam_w, out_w, lam_init, tq):
    batch, tq_total, _ = qh.shape
    tk = kt.shape[-1]
    return pl.pallas_call(
        functools.partial(_attn_kernel, lam_init=lam_init),
        grid=(batch, N_DIFF_HEADS, tq_total // tq),
        in_specs=[pl.BlockSpec((None, tq, HEAD_W), lambda b, h, i: (b, i, h)),
                  pl.BlockSpec((None, None, HEAD_W, tk), lambda b, h, i: (b, h, 0, 0)),
                  pl.BlockSpec((None, tk, HEAD_W), lambda b, h, i: (b, 0, h)),
                  pl.BlockSpec((4, DIFF_HEAD_DIM), lambda b, h, i: (0, 0)),
                  pl.BlockSpec((1, HEAD_W), lambda b, h, i: (0, 0))],
        out_specs=pl.BlockSpec((None, tq, HEAD_W), lambda b, h, i: (b, i, h)),
        out_shape=jax.ShapeDtypeStruct((batch, tq_total, DIFF_W), BF16),
        compiler_params=_cparams(("parallel", "parallel", "parallel"), 48),
        name="diff_attn",
    )(qh, kt, vh, lam_w, out_w)


def _gla_kernel(*refs, reverse, final, zero_init, n_chunks):
    if final:
        q_ref, k_ref, v_ref, la_ref, s0_ref, of_ref, gg_ref, gw_ref, o_ref, s_ref, state = refs
    else:
        q_ref, k_ref, v_ref, la_ref, s0_ref, o_ref, s_ref, state = refs
    t = pl.program_id(2)

    @pl.when(t == 0)
    def _():
        state[...] = jnp.zeros_like(state) if zero_init else s0_ref[...]

    r = lax.broadcasted_iota(jnp.int32, (GLA_CHUNK, GLA_CHUNK), 0)
    c = lax.broadcasted_iota(jnp.int32, (GLA_CHUNK, GLA_CHUNK), 1)
    tri = (c >= r) if reverse else (c <= r)
    tri_b = tri.astype(BF16)
    ones_b = jnp.ones((GLA_CHUNK, GLA_DK), BF16)
    contract_rows = (((0,), (0,)), ((), ()))
    chunks = range(n_chunks - 1, -1, -1) if reverse else range(n_chunks)
    for ci in chunks:
        rows = slice(ci * GLA_CHUNK, (ci + 1) * GLA_CHUNK)
        la = la_ref[rows, :]
        hi = la.astype(BF16)
        lo = (la - hi.astype(F32)).astype(BF16)
        b = _dot(tri_b, hi) + _dot(tri_b, lo)
        b_last = b[0:1] if reverse else b[GLA_CHUNK - 1:GLA_CHUNK]
        q = q_ref[rows, :] * GLA_DK ** -0.5
        k = k_ref[rows, :]
        v = v_ref[rows, :].astype(BF16)
        q_dec = (q * jnp.exp(b)).astype(BF16)
        k_inv = (k * jnp.exp(-b)).astype(BF16)
        k_end = (k * jnp.exp(b_last - b)).astype(BF16)
        att = lax.dot_general(q_dec, k_inv, (((1,), (1,)), ((), ())), preferred_element_type=F32)
        att = jnp.where(tri, att, 0.0).astype(BF16)
        s_prev = state[...]
        o = _dot(att, v) + _dot(q_dec, s_prev.astype(BF16))
        u = lax.dot_general(k_end, v, contract_rows, preferred_element_type=F32)
        tot = (lax.dot_general(hi, ones_b, contract_rows, preferred_element_type=F32)
               + lax.dot_general(lo, ones_b, contract_rows, preferred_element_type=F32))
        decay = jnp.exp(tot)
        state[...] = jnp.concatenate([decay, decay], axis=1) * s_prev + u
        if final:
            tot_o = o + of_ref[rows, :]
            y = tot_o * lax.rsqrt(jnp.mean(tot_o * tot_o, axis=-1, keepdims=True) + EPS) * gw_ref[...]
            gg = gg_ref[rows, :]
            y = y * (gg * (1.0 / (1.0 + jnp.exp(-gg))))
            o_ref[rows, :] = y.astype(BF16)
        else:
            o_ref[rows, :] = o

    @pl.when(t == pl.num_programs(2) - 1)
    def _():
        s_ref[...] = state[...]


def _gla_pass(proj, la, s0, o_fwd, gla_w, row0, batch, seq, direction, zero_init):
    tb = ROW_TILE
    nt = seq // tb
    reverse = direction == 1
    final = o_fwd is not None
    tt = (lambda t: nt - 1 - t) if reverse else (lambda t: t)
    rb0 = row0 // tb
    q_c0 = 3 * DIFF_W // GLA_DK
    k_c0 = (3 * DIFF_W + GLA_KW) // GLA_DK
    v_c0 = (3 * DIFF_W + 2 * GLA_KW) // GLA_DV
    g_c0 = (3 * DIFF_W + 2 * GLA_KW + GLA_VW) // GLA_DV
    la_c0 = direction * N_GLA_HEADS
    prow = lambda c0: (lambda b, h, t: (rb0 + b * nt + tt(t), c0 + h))
    lrow = lambda c0: (lambda b, h, t: (b * nt + tt(t), c0 + h))
    in_specs = [pl.BlockSpec((tb, GLA_DK), prow(q_c0)),
                pl.BlockSpec((tb, GLA_DK), prow(k_c0)),
                pl.BlockSpec((tb, GLA_DV), prow(v_c0)),
                pl.BlockSpec((tb, GLA_DK), prow(la_c0)),
                pl.BlockSpec((None, None, GLA_DK, GLA_DV), lambda b, h, t: (b, h, 0, 0))]
    args = [proj, proj, proj, la, s0]
    if final:
        in_specs += [pl.BlockSpec((tb, GLA_DV), lrow(0)),
                     pl.BlockSpec((tb, GLA_DV), prow(g_c0)),
                     pl.BlockSpec((1, GLA_DV), lambda b, h, t: (0, 0))]
        args += [o_fwd, proj, gla_w]
    n_rows = batch * seq
    return pl.pallas_call(
        functools.partial(_gla_kernel, reverse=reverse, final=final, zero_init=zero_init,
                          n_chunks=tb // GLA_CHUNK),
        grid=(batch, N_GLA_HEADS, nt),
        in_specs=in_specs,
        out_specs=[pl.BlockSpec((tb, GLA_DV), lrow(0)),
                   pl.BlockSpec((None, None, GLA_DK, GLA_DV), lambda b, h, t: (b, h, 0, 0))],
        out_shape=[jax.ShapeDtypeStruct((n_rows, GLA_VW), BF16 if final else F32),
                   jax.ShapeDtypeStruct((batch, N_GLA_HEADS, GLA_DK, GLA_DV), F32)],
        scratch_shapes=[pltpu.VMEM((GLA_DK, GLA_DV), F32)],
        compiler_params=_cparams(("parallel", "parallel", "arbitrary"), 40),
        name="gla_bwd" if reverse else "gla_fwd",
    )(*args)


def _gla(proj, la, s0_f, s0_b, gla_w, row0, batch, seq, zero_init):
    o_f, s_f = _gla_pass(proj, la, s0_f, None, None, row0, batch, seq, 0, zero_init)
    o, s_b = _gla_pass(proj, la, s0_b, o_f, gla_w, row0, batch, seq, 1, zero_init)
    return o, s_f, s_b


def _out_proj_kernel(x_ref, od_ref, og_ref, w_ref, ga_ref, nw_ref, sh_ref, sc_ref, rw_ref, rb_ref,
                     xo_ref, h_ref, idx_ref, gate_ref):
    mixed = _dot(od_ref[...], w_ref[0:DIFF_W, :]) + _dot(og_ref[...], w_ref[DIFF_W:, :])
    x = x_ref[...] + ga_ref[...] * mixed
    xo_ref[...] = x
    y = x * lax.rsqrt(jnp.mean(x * x, axis=-1, keepdims=True) + EPS) * nw_ref[...]
    hb = (y * (1.0 + sc_ref[...]) + sh_ref[...]).astype(BF16)
    h_ref[...] = hb
    logits = _dot(hb, rw_ref[...]) + rb_ref[...]
    lane = lax.broadcasted_iota(jnp.int32, logits.shape, 1)
    lane_f = lane.astype(F32)
    idx_out = jnp.zeros(logits.shape, F32)
    vals = []
    for kk in range(TOP_K):
        m = jnp.max(logits, axis=-1, keepdims=True)
        idx = jnp.min(jnp.where(logits == m, lane_f, float(LANES)), axis=-1, keepdims=True)
        logits = jnp.where(lane_f == idx, 2.0 * NEG_BIG, logits)
        idx_out = jnp.where(lane == kk, idx, idx_out)
        vals.append(m)
    es = [jnp.exp(vv - vals[0]) for vv in vals]
    denom = es[0] + es[1] + es[2] + es[3]
    gate_out = jnp.zeros(logits.shape, F32)
    for kk in range(TOP_K):
        gate_out = jnp.where(lane == kk, es[kk] / denom, gate_out)
    idx_ref[...] = idx_out.astype(jnp.int32)
    gate_ref[...] = gate_out


def _out_proj(x_all, o_diff, o_gla, w_out_b, mod_tab, norm_w, rw_pad, rb_pad, l, n_ctx, dec_seq):
    n, d = x_all.shape
    tm = ROW_TILE
    mi = lambda c: _mod_index(l, c, tm, n_ctx, dec_seq)
    return pl.pallas_call(
        _out_proj_kernel,
        grid=(n // tm,),
        in_specs=[pl.BlockSpec((tm, d), lambda i: (i, 0)),
                  pl.BlockSpec((tm, DIFF_W), lambda i: (i, 0)),
                  pl.BlockSpec((tm, GLA_VW), lambda i: (i, 0)),
                  pl.BlockSpec((None, d, d), lambda i: (l, 0, 0)),
                  pl.BlockSpec((None, 1, d), mi(2)),
                  pl.BlockSpec((None, 1, d), lambda i: (l, 0, 0)),
                  pl.BlockSpec((None, 1, d), mi(3)),
                  pl.BlockSpec((None, 1, d), mi(4)),
                  pl.BlockSpec((None, d, LANES), lambda i: (l, 0, 0)),
                  pl.BlockSpec((None, 1, LANES), lambda i: (l, 0, 0))],
        out_specs=[pl.BlockSpec((tm, d), lambda i: (i, 0)),
                   pl.BlockSpec((tm, d), lambda i: (i, 0)),
                   pl.BlockSpec((tm, LANES), lambda i: (i, 0)),
                   pl.BlockSpec((tm, LANES), lambda i: (i, 0))],
        out_shape=[jax.ShapeDtypeStruct((n, d), F32),
                   jax.ShapeDtypeStruct((n, d), BF16),
                   jax.ShapeDtypeStruct((n, LANES), jnp.int32),
                   jax.ShapeDtypeStruct((n, LANES), F32)],
        compiler_params=_cparams(("parallel",), 48),
        name="out_proj_router",
    )(x_all, o_diff, o_gla, w_out_b, mod_tab, norm_w, mod_tab, mod_tab, rw_pad, rb_pad)


def _expert_changed(be_ref, m):
    return jnp.logical_or(m == 0, be_ref[m] != be_ref[jnp.maximum(m - 1, 0)])


def _moe_up_kernel(be_ref, nu_ref, x_ref, wg_ref, wl_ref, bg_ref, bl_ref, o_ref, wg_scr, wl_scr):
    m = pl.program_id(1)

    @pl.when(_expert_changed(be_ref, m))
    def _():
        wg_scr[...] = wg_ref[...].astype(BF16)
        wl_scr[...] = wl_ref[...].astype(BF16)

    @pl.when(m < nu_ref[0])
    def _():
        x = x_ref[...]
        glu = jnp.minimum(_dot(x, wg_scr[...]) + bg_ref[...], SWIGLU_LIMIT)
        lin = jnp.clip(_dot(x, wl_scr[...]) + bl_ref[...], -SWIGLU_LIMIT, SWIGLU_LIMIT)
        act = glu * (1.0 / (1.0 + jnp.exp(-SWIGLU_ALPHA * glu))) * (lin + 1.0)
        o_ref[...] = act.astype(BF16)

    @pl.when(m >= nu_ref[0])
    def _():
        o_ref[...] = jnp.zeros_like(o_ref)


def _moe_down_kernel(be_ref, nu_ref, a_ref, w_ref, b_ref, o_ref, w_scr):
    m = pl.program_id(1)

    @pl.when(_expert_changed(be_ref, m))
    def _():
        w_scr[...] = w_ref[...].astype(BF16)

    @pl.when(m < nu_ref[0])
    def _():
        o_ref[...] = _dot(a_ref[...], w_scr[...]) + b_ref[...]

    @pl.when(m >= nu_ref[0])
    def _():
        o_ref[...] = jnp.zeros_like(o_ref)


def _experts(xb, block_e, n_used, w_gu, b_gu, w_dn, b_dn, l):
    rows, d = xb.shape
    bm = MOE_ROWS
    nb = rows // bm
    tf = 512
    nf = D_FF // tf
    act = pl.pallas_call(
        _moe_up_kernel,
        grid_spec=pltpu.PrefetchScalarGridSpec(
            num_scalar_prefetch=2,
            grid=(nf, nb),
            in_specs=[pl.BlockSpec((bm, d), lambda f, m, be, nu: (m, 0)),
                      pl.BlockSpec((None, None, d, tf), lambda f, m, be, nu: (l, be[m], 0, f)),
                      pl.BlockSpec((None, None, d, tf), lambda f, m, be, nu: (l, be[m], 0, nf + f)),
                      pl.BlockSpec((None, None, 1, tf), lambda f, m, be, nu: (l, be[m], 0, f)),
                      pl.BlockSpec((None, None, 1, tf), lambda f, m, be, nu: (l, be[m], 0, nf + f))],
            out_specs=pl.BlockSpec((bm, tf), lambda f, m, be, nu: (m, f)),
            scratch_shapes=[pltpu.VMEM((d, tf), BF16), pltpu.VMEM((d, tf), BF16)]),
        out_shape=jax.ShapeDtypeStruct((rows, D_FF), BF16),
        compiler_params=_cparams(("arbitrary", "arbitrary"), 48),
        name="moe_up",
    )(block_e, n_used, xb, w_gu, w_gu, b_gu, b_gu)
    tn = 512
    nn = d // tn
    return pl.pallas_call(
        _moe_down_kernel,
        grid_spec=pltpu.PrefetchScalarGridSpec(
            num_scalar_prefetch=2,
            grid=(nn, nb),
            in_specs=[pl.BlockSpec((bm, D_FF), lambda n, m, be, nu: (m, 0)),
                      pl.BlockSpec((None, None, D_FF, tn), lambda n, m, be, nu: (l, be[m], 0, n)),
                      pl.BlockSpec((None, None, 1, tn), lambda n, m, be, nu: (l, be[m], 0, n))],
            out_specs=pl.BlockSpec((bm, tn), lambda n, m, be, nu: (m, n)),
            scratch_shapes=[pltpu.VMEM((D_FF, tn), BF16)]),
        out_shape=jax.ShapeDtypeStruct((rows, d), F32),
        compiler_params=_cparams(("arbitrary", "arbitrary"), 48),
        name="moe_down",
    )(block_e, n_used, act, w_dn, b_dn)


def _route(top_idx):
    n_tok = top_idx.shape[0]
    n_items = n_tok * TOP_K
    bm = MOE_ROWS
    flat_e = top_idx.reshape(-1)
    order = jnp.argsort(flat_e)
    sorted_e = flat_e[order]
    counts = jnp.bincount(flat_e, length=N_EXPERTS)
    starts = jnp.cumsum(counts) - counts
    padded = (counts + bm - 1) // bm * bm
    pad_end = jnp.cumsum(padded)
    pad_start = pad_end - padded
    dest = (pad_start[sorted_e] + jnp.arange(n_items, dtype=jnp.int32) - starts[sorted_e]).astype(jnp.int32)
    n_blocks = -(-n_items // bm) + N_EXPERTS
    buf_tok = jnp.zeros((n_blocks * bm,), jnp.int32).at[dest].set((order // TOP_K).astype(jnp.int32))
    n_used = (pad_end[-1] // bm).astype(jnp.int32)
    blk = jnp.minimum(jnp.arange(n_blocks, dtype=jnp.int32), n_used - 1) * bm
    block_e = jnp.minimum(jnp.searchsorted(pad_end, blk, side='right'), N_EXPERTS - 1).astype(jnp.int32)
    item_dest = jnp.zeros((n_items,), jnp.int32).at[order].set(dest)
    return buf_tok, block_e, n_used.reshape(1), item_dest


def _combine_kernel(x_ref, y_ref, g_ref, gf_ref, o_ref):
    g = g_ref[...]
    acc = jnp.zeros(x_ref.shape, F32)
    for kk in range(TOP_K):
        acc = acc + y_ref[:, kk * D_MODEL:(kk + 1) * D_MODEL] * g[:, kk:kk + 1]
    o_ref[...] = x_ref[...] + gf_ref[...] * acc


def _combine(x_all, y_items, gates, mod_tab, l, n_ctx, dec_seq):
    n, d = x_all.shape
    tm = ROW_TILE
    return pl.pallas_call(
        _combine_kernel,
        grid=(n // tm,),
        in_specs=[pl.BlockSpec((tm, d), lambda i: (i, 0)),
                  pl.BlockSpec((tm, TOP_K * d), lambda i: (i, 0)),
                  pl.BlockSpec((tm, LANES), lambda i: (i, 0)),
                  pl.BlockSpec((None, 1, d), _mod_index(l, 5, tm, n_ctx, dec_seq))],
        out_specs=pl.BlockSpec((tm, d), lambda i: (i, 0)),
        out_shape=jax.ShapeDtypeStruct((n, d), F32),
        compiler_params=_cparams(("parallel",), 48),
        name="moe_combine",
    )(x_all, y_items, gates, mod_tab)


def kernel(x_prompt, x_sample, cache_diff_k, cache_diff_v, state_gla, c, c_ctx, norm_attn_w, norm_ffn_w, w_ada, b_ada, w_in, q_norm_w, k_norm_w, diff_lambda_w, diff_out_norm_w, w_gla_a2, b_gla_a, gla_out_norm_w, w_out, router_w, router_b, w_gate_up, b_gate_up, w_down, b_down):
    batch, seq, d = x_prompt.shape
    dec_batch, dec_seq, _ = x_sample.shape
    depth = w_in.shape[0]
    past = cache_diff_k.shape[2]
    n_ctx = batch * seq
    n_lat = dec_batch * dec_seq
    assert d == D_MODEL and 1 + dec_batch <= 8
    assert seq % ROW_TILE == 0 and dec_seq % ROW_TILE == 0 and past % ROW_TILE == 0
    tm_proj = 512 if (n_ctx % 512 == 0 and dec_seq % 512 == 0) else ROW_TILE

    cond8 = jnp.zeros((8, d), F32).at[0].set(c_ctx).at[1:1 + dec_batch].set(c)
    w_main = w_in[:, :, :PROJ_W].astype(BF16)
    w_ga = w_in[:, :, PROJ_W:].reshape(depth, d, 2, GLA_GATE_RANK).transpose(0, 2, 1, 3)
    w_ga = jnp.pad(w_ga, ((0, 0), (0, 0), (0, 0), (0, LANES - GLA_GATE_RANK))).astype(BF16)
    a2_pad = jnp.pad(w_gla_a2, ((0, 0), (0, 0), (0, LANES - GLA_GATE_RANK), (0, 0)))
    b_a = b_gla_a.reshape(depth, 2, 1, GLA_KW)
    qw = jnp.tile(q_norm_w, (1, 2)).reshape(depth, 1, HEAD_W)
    kw = jnp.tile(k_norm_w, (1, 2)).reshape(depth, 1, HEAD_W)
    w_out_b = w_out.astype(BF16)
    rw_pad = jnp.pad(router_w, ((0, 0), (0, 0), (0, LANES - N_EXPERTS))).astype(BF16)
    rb_pad = jnp.pad(router_b, ((0, 0), (0, LANES - N_EXPERTS)), constant_values=NEG_BIG).reshape(depth, 1, LANES)
    b_gu = b_gate_up.reshape(depth, N_EXPERTS, 1, 2 * D_FF)
    b_dn = b_down.reshape(depth, N_EXPERTS, 1, d)
    cos_t, sin_t = _rope_tables(dec_seq)
    cache_k = cache_diff_k.reshape(dec_batch, depth, past, DIFF_W)
    cache_v = cache_diff_v.reshape(dec_batch, depth, past, DIFF_W)

    mod_tab = _adaln(cond8, w_ada, b_ada).reshape(depth * 8 * 6, 1, d)
    x_all = jnp.concatenate([x_prompt.reshape(n_ctx, d), x_sample.reshape(n_lat, d)], axis=0)
    zero_state = jnp.zeros((batch, N_GLA_HEADS, GLA_DK, GLA_DV), F32)

    ks, vs, sts = [], [], []
    for l in range(depth):
        lam_init = 0.8 - 0.6 * math.exp(-0.3 * l)
        proj, la = _in_proj(x_all, norm_attn_w.reshape(depth, 1, d), mod_tab, w_main, w_ga, a2_pad, b_a,
                            l, n_ctx, dec_seq, tm_proj)
        qh, kt, vh, kc = _prep_ctx(proj, qw[l], kw[l], batch, seq)
        od_ctx = _attention(qh, kt, vh, diff_lambda_w[l], diff_out_norm_w[l].reshape(1, HEAD_W),
                            lam_init, ROW_TILE)
        og_ctx, s_f, s_b = _gla(proj, la, zero_state, zero_state, gla_out_norm_w[l].reshape(1, GLA_DV),
                                0, batch, seq, True)
        ks.append(kc.reshape(batch, seq, N_DIFF_HEADS, 2, DIFF_HEAD_DIM))
        vs.append(proj[:n_ctx, 2 * DIFF_W:3 * DIFF_W].reshape(batch, seq, N_DIFF_HEADS, HEAD_W))
        sts.append(jnp.stack([s_f, s_b], axis=1))
        qh, kt, vh = _prep_lat(proj, cache_k, cache_v, qw[l], kw[l], cos_t, sin_t, l, n_ctx,
                               dec_batch, dec_seq, past)
        od_lat = _attention(qh, kt, vh, diff_lambda_w[l], diff_out_norm_w[l].reshape(1, HEAD_W),
                            lam_init, ROW_TILE)
        og_lat, _, _ = _gla(proj, la, state_gla[:, l, 0], state_gla[:, l, 1],
                            gla_out_norm_w[l].reshape(1, GLA_DV), n_ctx, dec_batch, dec_seq, False)
        o_diff = jnp.concatenate([od_ctx.reshape(n_ctx, DIFF_W), od_lat.reshape(n_lat, DIFF_W)], axis=0)
        o_gla = jnp.concatenate([og_ctx, og_lat], axis=0)
        x_all, h2, top_idx, gates = _out_proj(x_all, o_diff, o_gla, w_out_b, mod_tab,
                                              norm_ffn_w.reshape(depth, 1, d), rw_pad, rb_pad,
                                              l, n_ctx, dec_seq)
        buf_tok, block_e, n_used, item_dest = _route(top_idx[:, :TOP_K])
        xb = h2[buf_tok]
        yb = _experts(xb, block_e, n_used, w_gate_up, b_gu, w_down, b_dn, l)
        y_items = yb[item_dest].reshape(n_ctx + n_lat, TOP_K * d)
        x_all = _combine(x_all, y_items, gates, mod_tab, l, n_ctx, dec_seq)

    y_prompt = x_all[:n_ctx].reshape(batch, seq, d)
    y_sample = x_all[n_ctx:].reshape(dec_batch, dec_seq, d)
    return (y_prompt, y_sample, jnp.stack(ks, axis=1), jnp.stack(vs, axis=1), jnp.stack(sts, axis=1))
```

```python
import functools
import math

import jax
import jax.numpy as jnp
from jax import lax
from jax.experimental import pallas as pl
from jax.experimental.pallas import tpu as pltpu

F32 = jnp.float32
BF16 = jnp.bfloat16

D_MODEL = 2048
GRID_W = 64
N_DIFF_HEADS = 8
DIFF_HEAD_DIM = 64
HEAD_W = 2 * DIFF_HEAD_DIM
ROPE_AXIS_DIM = DIFF_HEAD_DIM // 2
ROPE_BASE = 10000.0
N_GLA_HEADS = 4
GLA_DK = 128
GLA_DV = 256
GLA_GATE_RANK = 16
GLA_GATE_TAU = 16.0
GLA_CHUNK = 64
N_EXPERTS = 32
TOP_K = 4
D_FF = D_MODEL
SWIGLU_LIMIT = 7.0
SWIGLU_ALPHA = 1.702
EPS = 1e-6

DIFF_W = N_DIFF_HEADS * HEAD_W
GLA_KW = N_GLA_HEADS * GLA_DK
GLA_VW = N_GLA_HEADS * GLA_DV
PROJ_W = 3 * DIFF_W + 2 * GLA_KW + 2 * GLA_VW
LANES = 128
ROW_TILE = 256
MOE_ROWS = 1024
MOE_SUB = 256
MOE_FF_TILE = 512
NEG_BIG = -1e30


def _cparams(sem, vmem_mb):
    return pltpu.CompilerParams(dimension_semantics=sem, vmem_limit_bytes=vmem_mb * 2**20)


def _dot(a, b):
    return jnp.dot(a, b, preferred_element_type=F32)


def _split_dot(a, b, dims=None):
    hi = a.astype(BF16)
    lo = (a - hi.astype(F32)).astype(BF16)
    if dims is None:
        return _dot(hi, b) + _dot(lo, b)
    return (lax.dot_general(hi, b, dims, preferred_element_type=F32)
            + lax.dot_general(lo, b, dims, preferred_element_type=F32))


def _adaln_kernel(c_ref, w_ref, b_ref, o_ref):
    c = c_ref[...]
    s = c * (1.0 / (1.0 + jnp.exp(-c)))
    o_ref[...] = _dot(s.astype(BF16), w_ref[...].astype(BF16)) + b_ref[...]


def _adaln(cond8, w_ada, b_ada):
    depth, d, n = w_ada.shape
    tn = 1024
    return pl.pallas_call(
        _adaln_kernel,
        grid=(depth, n // tn),
        in_specs=[pl.BlockSpec((8, d), lambda l, j: (0, 0)),
                  pl.BlockSpec((None, d, tn), lambda l, j: (l, 0, j)),
                  pl.BlockSpec((None, 1, tn), lambda l, j: (l, 0, j))],
        out_specs=pl.BlockSpec((None, 8, tn), lambda l, j: (l, 0, j)),
        out_shape=jax.ShapeDtypeStruct((depth, 8, n), F32),
        compiler_params=_cparams(("parallel", "parallel"), 40),
        name="adaln",
    )(cond8, w_ada, b_ada.reshape(depth, 1, n))


def _mod_index(l, chunk, tm, n_ctx, dec_seq):
    def index_map(i, *_):
        r = jnp.where(i * tm < n_ctx, 0, 1 + (i * tm - n_ctx) // dec_seq)
        return ((l * 8 + r) * 6 + chunk, 0, 0)
    return index_map


def _in_proj_kernel(x_ref, nw_ref, sh_ref, sc_ref, w_ref, wga_ref, a2_ref, ba_ref,
                    o_ref, la_ref, h_scr):
    @pl.when(pl.program_id(1) == 0)
    def _():
        x = x_ref[...]
        y = x * lax.rsqrt(jnp.mean(x * x, axis=-1, keepdims=True) + EPS) * nw_ref[...]
        hb = (y * (1.0 + sc_ref[...]) + sh_ref[...]).astype(BF16)
        h_scr[...] = hb
        for m in range(2):
            ga = _dot(hb, wga_ref[m])
            z = _dot(ga.astype(BF16), a2_ref[m].astype(BF16)) + ba_ref[m]
            ls = jnp.minimum(z, 0.0) - jnp.log1p(jnp.exp(-jnp.abs(z)))
            la_ref[:, m * GLA_KW:(m + 1) * GLA_KW] = ls * (1.0 / GLA_GATE_TAU)

    o_ref[...] = _dot(h_scr[...], w_ref[...])


def _in_proj(x_all, norm_w, mod_tab, w_main, w_ga, a2_pad, b_a, l, n_ctx, dec_seq, tm):
    n, d = x_all.shape
    tn = 1024
    return pl.pallas_call(
        _in_proj_kernel,
        grid=(n // tm, PROJ_W // tn),
        in_specs=[pl.BlockSpec((tm, d), lambda i, j: (i, 0)),
                  pl.BlockSpec((None, 1, d), lambda i, j: (l, 0, 0)),
                  pl.BlockSpec((None, 1, d), _mod_index(l, 0, tm, n_ctx, dec_seq)),
                  pl.BlockSpec((None, 1, d), _mod_index(l, 1, tm, n_ctx, dec_seq)),
                  pl.BlockSpec((None, d, tn), lambda i, j: (l, 0, j)),
                  pl.BlockSpec((None, 2, d, LANES), lambda i, j: (l, 0, 0, 0)),
                  pl.BlockSpec((None, 2, LANES, GLA_KW), lambda i, j: (l, 0, 0, 0)),
                  pl.BlockSpec((None, 2, 1, GLA_KW), lambda i, j: (l, 0, 0, 0))],
        out_specs=[pl.BlockSpec((tm, tn), lambda i, j: (i, j)),
                   pl.BlockSpec((tm, 2 * GLA_KW), lambda i, j: (i, 0))],
        out_shape=[jax.ShapeDtypeStruct((n, PROJ_W), F32),
                   jax.ShapeDtypeStruct((n, 2 * GLA_KW), F32)],
        scratch_shapes=[pltpu.VMEM((tm, d), BF16)],
        compiler_params=_cparams(("parallel", "arbitrary"), 48),
        name="in_proj",
    )(x_all, norm_w, mod_tab, mod_tab, w_main, w_ga, a2_pad, b_a)


def _group_ones():
    r = lax.broadcasted_iota(jnp.int32, (HEAD_W, HEAD_W), 0) // DIFF_HEAD_DIM
    c = lax.broadcasted_iota(jnp.int32, (HEAD_W, HEAD_W), 1) // DIFF_HEAD_DIM
    return (r == c).astype(BF16)


def _head_norm(x, w, g):
    ss = _split_dot(x * x, g)
    return x * lax.rsqrt(ss * (1.0 / DIFF_HEAD_DIM) + EPS) * w


def _rope(y, cos, sin_signed, lo_mask):
    half = ROPE_AXIS_DIM // 2
    partner = jnp.where(lo_mask, pltpu.roll(y, HEAD_W - half, 1), pltpu.roll(y, half, 1))
    return y * cos + partner * sin_signed


def _prep_ctx_kernel(q_ref, k_ref, v_ref, qw_ref, kw_ref, qh_ref, kt_ref, vh_ref, kc_ref):
    g = _group_ones()
    for h in range(N_DIFF_HEADS):
        sl = slice(h * HEAD_W, (h + 1) * HEAD_W)
        qn = _head_norm(q_ref[:, sl], qw_ref[...], g)
        kn = _head_norm(k_ref[:, sl], kw_ref[...], g)
        qh_ref[:, sl] = (qn * DIFF_HEAD_DIM ** -0.5).astype(BF16)
        kc_ref[:, sl] = kn
        kt_ref[h] = kn.T.astype(BF16)
    vh_ref[...] = v_ref[...].astype(BF16)


def _prep_ctx(proj, qw, kw, batch, seq):
    tm = ROW_TILE
    nt = seq // tm
    row = lambda c: (lambda b, j: (b * nt + j, c))
    return pl.pallas_call(
        _prep_ctx_kernel,
        grid=(batch, nt),
        in_specs=[pl.BlockSpec((tm, DIFF_W), row(0)),
                  pl.BlockSpec((tm, DIFF_W), row(1)),
                  pl.BlockSpec((tm, DIFF_W), row(2)),
                  pl.BlockSpec((1, HEAD_W), lambda b, j: (0, 0)),
                  pl.BlockSpec((1, HEAD_W), lambda b, j: (0, 0))],
        out_specs=[pl.BlockSpec((None, tm, DIFF_W), lambda b, j: (b, j, 0)),
                   pl.BlockSpec((None, N_DIFF_HEADS, HEAD_W, tm), lambda b, j: (b, 0, 0, j)),
                   pl.BlockSpec((None, tm, DIFF_W), lambda b, j: (b, j, 0)),
                   pl.BlockSpec((None, tm, DIFF_W), lambda b, j: (b, j, 0))],
        out_shape=[jax.ShapeDtypeStruct((batch, seq, DIFF_W), BF16),
                   jax.ShapeDtypeStruct((batch, N_DIFF_HEADS, HEAD_W, seq), BF16),
                   jax.ShapeDtypeStruct((batch, seq, DIFF_W), BF16),
                   jax.ShapeDtypeStruct((batch, seq, DIFF_W), F32)],
        compiler_params=_cparams(("parallel", "parallel"), 40),
        name="prep_ctx",
    )(proj, proj, proj, qw, kw)


def _prep_lat_kernel(q_ref, k_ref, v_ref, ck_ref, cv_ref, qw_ref, kw_ref, cos_ref, sin_ref,
                     qh_ref, kt_ref, vh_ref, *, n_lat_tiles):
    j = pl.program_id(1)

    @pl.when(j < n_lat_tiles)
    def _():
        g = _group_ones()
        cos = cos_ref[...]
        sin = sin_ref[...]
        lane = lax.broadcasted_iota(jnp.int32, cos.shape, 1)
        lo_mask = (lane % ROPE_AXIS_DIM) < (ROPE_AXIS_DIM // 2)
        for h in range(N_DIFF_HEADS):
            sl = slice(h * HEAD_W, (h + 1) * HEAD_W)
            qn = _rope(_head_norm(q_ref[:, sl], qw_ref[...], g), cos, sin, lo_mask)
            kn = _rope(_head_norm(k_ref[:, sl], kw_ref[...], g), cos, sin, lo_mask)
            qh_ref[:, sl] = (qn * DIFF_HEAD_DIM ** -0.5).astype(BF16)
            kt_ref[h] = kn.T.astype(BF16)
        vh_ref[...] = v_ref[...].astype(BF16)

    @pl.when(j >= n_lat_tiles)
    def _():
        for h in range(N_DIFF_HEADS):
            kt_ref[h] = ck_ref[:, h * HEAD_W:(h + 1) * HEAD_W].T.astype(BF16)
        vh_ref[...] = cv_ref[...].astype(BF16)


def _prep_lat(proj, cache_k, cache_v, qw, kw, cos_t, sin_t, l, n_ctx, dec_batch, dec_seq, past):
    tm = ROW_TILE
    nl = dec_seq // tm
    npast = past // tm
    row0 = n_ctx // tm
    row = lambda c: (lambda b, j: (row0 + b * nl + jnp.minimum(j, nl - 1), c))
    cache = lambda b, j: (b, l, jnp.maximum(j - nl, 0), 0)
    tab = lambda b, j: (jnp.minimum(j, nl - 1), 0)
    return pl.pallas_call(
        functools.partial(_prep_lat_kernel, n_lat_tiles=nl),
        grid=(dec_batch, nl + npast),
        in_specs=[pl.BlockSpec((tm, DIFF_W), row(0)),
                  pl.BlockSpec((tm, DIFF_W), row(1)),
                  pl.BlockSpec((tm, DIFF_W), row(2)),
                  pl.BlockSpec((None, None, tm, DIFF_W), cache),
                  pl.BlockSpec((None, None, tm, DIFF_W), cache),
                  pl.BlockSpec((1, HEAD_W), lambda b, j: (0, 0)),
                  pl.BlockSpec((1, HEAD_W), lambda b, j: (0, 0)),
                  pl.BlockSpec((tm, HEAD_W), tab),
                  pl.BlockSpec((tm, HEAD_W), tab)],
        out_specs=[pl.BlockSpec((None, tm, DIFF_W), lambda b, j: (b, jnp.minimum(j, nl - 1), 0)),
                   pl.BlockSpec((None, N_DIFF_HEADS, HEAD_W, tm), lambda b, j: (b, 0, 0, j)),
                   pl.BlockSpec((None, tm, DIFF_W), lambda b, j: (b, j, 0))],
        out_shape=[jax.ShapeDtypeStruct((dec_batch, dec_seq, DIFF_W), BF16),
                   jax.ShapeDtypeStruct((dec_batch, N_DIFF_HEADS, HEAD_W, dec_seq + past), BF16),
                   jax.ShapeDtypeStruct((dec_batch, dec_seq + past, DIFF_W), BF16)],
        compiler_params=_cparams(("parallel", "arbitrary"), 40),
        name="prep_lat",
    )(proj, proj, proj, cache_k, cache_v, qw, kw, cos_t, sin_t)


def _rope_tables(n_tok):
    t = jnp.arange(n_tok)
    r = (t // GRID_W).astype(F32)
    c = (t % GRID_W).astype(F32)
    inv = ROPE_BASE ** (-jnp.arange(0, ROPE_AXIS_DIM, 2, dtype=F32) / ROPE_AXIS_DIM)
    ar = r[:, None] * inv
    ac = c[:, None] * inv
    cos64 = jnp.concatenate([jnp.cos(ar), jnp.cos(ar), jnp.cos(ac), jnp.cos(ac)], axis=-1)
    sin64 = jnp.concatenate([-jnp.sin(ar), jnp.sin(ar), -jnp.sin(ac), jnp.sin(ac)], axis=-1)
    return jnp.tile(cos64, (1, 2)), jnp.tile(sin64, (1, 2))


def _attn_kernel(q_ref, kt_ref, v_ref, lw_ref, ow_ref, o_ref, *, lam_init):
    q = q_ref[...]
    lane = lax.broadcasted_iota(jnp.int32, q.shape, 1)
    kt = kt_ref[...]
    v = v_ref[...]

    def attend(qm):
        s = _dot(qm, kt)
        p = jnp.exp(s - jnp.max(s, axis=-1, keepdims=True))
        denom = jnp.sum(p, axis=-1, keepdims=True)
        return _dot(p.astype(BF16), v) / denom

    o1 = attend(jnp.where(lane < DIFF_HEAD_DIM, q, jnp.zeros_like(q)))
    o2 = attend(jnp.where(lane >= DIFF_HEAD_DIM, q, jnp.zeros_like(q)))
    lw = lw_ref[...]
    lam = (jnp.exp(jnp.sum(lw[0:1] * lw[1:2], axis=-1, keepdims=True))
           - jnp.exp(jnp.sum(lw[2:3] * lw[3:4], axis=-1, keepdims=True)) + lam_init)
    o = o1 - lam * o2
    y = o * lax.rsqrt(jnp.mean(o * o, axis=-1, keepdims=True) + EPS) * ow_ref[...]
    o_ref[...] = (y * (1.0 - lam_init)).astype(BF16)


def _attention(qh, kt, vh, lam_w, out_w, lam_init, tq):
    batch, tq_total, _ = qh.shape
    tk = kt.shape[-1]
    return pl.pallas_call(
        functools.partial(_attn_kernel, lam_init=lam_init),
        grid=(batch, N_DIFF_HEADS, tq_total // tq),
        in_specs=[pl.BlockSpec((None, tq, HEAD_W), lambda b, h, i: (b, i, h)),
                  pl.BlockSpec((None, None, HEAD_W, tk), lambda b, h, i: (b, h, 0, 0)),
                  pl.BlockSpec((None, tk, HEAD_W), lambda b, h, i: (b, 0, h)),
                  pl.BlockSpec((4, DIFF_HEAD_DIM), lambda b, h, i: (0, 0)),
                  pl.BlockSpec((1, HEAD_W), lambda b, h, i: (0, 0))],
        out_specs=pl.BlockSpec((None, tq, HEAD_W), lambda b, h, i: (b, i, h)),
        out_shape=jax.ShapeDtypeStruct((batch, tq_total, DIFF_W), BF16),
        compiler_params=_cparams(("parallel", "parallel", "parallel"), 48),
        name="diff_attn",
    )(qh, kt, vh, lam_w, out_w)


def _gla_kernel(*refs, reverse, final, zero_init, n_chunks):
    if final:
        q_ref, k_ref, v_ref, la_ref, s0_ref, of_ref, gg_ref, gw_ref, o_ref, s_ref, state = refs
    else:
        q_ref, k_ref, v_ref, la_ref, s0_ref, o_ref, s_ref, state = refs
    t = pl.program_id(2)

    @pl.when(t == 0)
    def _():
        state[...] = jnp.zeros_like(state) if zero_init else s0_ref[...]

    r = lax.broadcasted_iota(jnp.int32, (GLA_CHUNK, GLA_CHUNK), 0)
    c = lax.broadcasted_iota(jnp.int32, (GLA_CHUNK, GLA_CHUNK), 1)
    tri = (c >= r) if reverse else (c <= r)
    tri_b = tri.astype(BF16)
    ones_b = jnp.ones((GLA_CHUNK, GLA_DK), BF16)
    contract_rows = (((0,), (0,)), ((), ()))
    chunks = range(n_chunks - 1, -1, -1) if reverse else range(n_chunks)
    for ci in chunks:
        rows = slice(ci * GLA_CHUNK, (ci + 1) * GLA_CHUNK)
        la = la_ref[rows, :]
        hi = la.astype(BF16)
        lo = (la - hi.astype(F32)).astype(BF16)
        b = _dot(tri_b, hi) + _dot(tri_b, lo)
        b_last = b[0:1] if reverse else b[GLA_CHUNK - 1:GLA_CHUNK]
        q = q_ref[rows, :] * GLA_DK ** -0.5
        k = k_ref[rows, :]
        v = v_ref[rows, :].astype(BF16)
        q_dec = (q * jnp.exp(b)).astype(BF16)
        k_inv = (k * jnp.exp(-b)).astype(BF16)
        k_end = (k * jnp.exp(b_last - b)).astype(BF16)
        att = lax.dot_general(q_dec, k_inv, (((1,), (1,)), ((), ())), preferred_element_type=F32)
        att = jnp.where(tri, att, 0.0).astype(BF16)
        s_prev = state[...]
        o = _dot(att, v) + _dot(q_dec, s_prev.astype(BF16))
        u = lax.dot_general(k_end, v, contract_rows, preferred_element_type=F32)
        tot = (lax.dot_general(hi, ones_b, contract_rows, preferred_element_type=F32)
               + lax.dot_general(lo, ones_b, contract_rows, preferred_element_type=F32))
        decay = jnp.exp(tot)
        state[...] = jnp.concatenate([decay, decay], axis=1) * s_prev + u
        if final:
            tot_o = o + of_ref[rows, :]
            y = tot_o * lax.rsqrt(jnp.mean(tot_o * tot_o, axis=-1, keepdims=True) + EPS) * gw_ref[...]
            gg = gg_ref[rows, :]
            y = y * (gg * (1.0 / (1.0 + jnp.exp(-gg))))
            o_ref[rows, :] = y.astype(BF16)
        else:
            o_ref[rows, :] = o

    @pl.when(t == pl.num_programs(2) - 1)
    def _():
        s_ref[...] = state[...]


def _gla_pass(proj, la, s0, o_fwd, gla_w, row0, batch, seq, direction, zero_init):
    tb = ROW_TILE
    nt = seq // tb
    reverse = direction == 1
    final = o_fwd is not None
    tt = (lambda t: nt - 1 - t) if reverse else (lambda t: t)
    rb0 = row0 // tb
    q_c0 = 3 * DIFF_W // GLA_DK
    k_c0 = (3 * DIFF_W + GLA_KW) // GLA_DK
    v_c0 = (3 * DIFF_W + 2 * GLA_KW) // GLA_DV
    g_c0 = (3 * DIFF_W + 2 * GLA_KW + GLA_VW) // GLA_DV
    la_c0 = direction * N_GLA_HEADS
    prow = lambda c0: (lambda b, h, t: (rb0 + b * nt + tt(t), c0 + h))
    lrow = lambda c0: (lambda b, h, t: (b * nt + tt(t), c0 + h))
    in_specs = [pl.BlockSpec((tb, GLA_DK), prow(q_c0)),
                pl.BlockSpec((tb, GLA_DK), prow(k_c0)),
                pl.BlockSpec((tb, GLA_DV), prow(v_c0)),
                pl.BlockSpec((tb, GLA_DK), prow(la_c0)),
                pl.BlockSpec((None, None, GLA_DK, GLA_DV), lambda b, h, t: (b, h, 0, 0))]
    args = [proj, proj, proj, la, s0]
    if final:
        in_specs += [pl.BlockSpec((tb, GLA_DV), lrow(0)),
                     pl.BlockSpec((tb, GLA_DV), prow(g_c0)),
                     pl.BlockSpec((1, GLA_DV), lambda b, h, t: (0, 0))]
        args += [o_fwd, proj, gla_w]
    n_rows = batch * seq
    return pl.pallas_call(
        functools.partial(_gla_kernel, reverse=reverse, final=final, zero_init=zero_init,
                          n_chunks=tb // GLA_CHUNK),
        grid=(batch, N_GLA_HEADS, nt),
        in_specs=in_specs,
        out_specs=[pl.BlockSpec((tb, GLA_DV), lrow(0)),
                   pl.BlockSpec((None, None, GLA_DK, GLA_DV), lambda b, h, t: (b, h, 0, 0))],
        out_shape=[jax.ShapeDtypeStruct((n_rows, GLA_VW), BF16 if final else F32),
                   jax.ShapeDtypeStruct((batch, N_GLA_HEADS, GLA_DK, GLA_DV), F32)],
        scratch_shapes=[pltpu.VMEM((GLA_DK, GLA_DV), F32)],
        compiler_params=_cparams(("parallel", "parallel", "arbitrary"), 40),
        name="gla_bwd" if reverse else "gla_fwd",
    )(*args)


def _gla(proj, la, s0_f, s0_b, gla_w, row0, batch, seq, zero_init):
    o_f, s_f = _gla_pass(proj, la, s0_f, None, None, row0, batch, seq, 0, zero_init)
    o, s_b = _gla_pass(proj, la, s0_b, o_f, gla_w, row0, batch, seq, 1, zero_init)
    return o, s_f, s_b


def _out_proj_kernel(x_ref, od_ref, og_ref, w_ref, ga_ref, nw_ref, sh_ref, sc_ref, rw_ref, rb_ref,
                     xo_ref, h_ref, idx_ref, gate_ref):
    mixed = _dot(od_ref[...], w_ref[0:DIFF_W, :]) + _dot(og_ref[...], w_ref[DIFF_W:, :])
    x = x_ref[...] + ga_ref[...] * mixed
    xo_ref[...] = x
    y = x * lax.rsqrt(jnp.mean(x * x, axis=-1, keepdims=True) + EPS) * nw_ref[...]
    h = y * (1.0 + sc_ref[...]) + sh_ref[...]
    h_ref[...] = h
    logits = _dot(h.astype(BF16), rw_ref[...]) + rb_ref[...]
    lane = lax.broadcasted_iota(jnp.int32, logits.shape, 1)
    lane_f = lane.astype(F32)
    idx_out = jnp.zeros(logits.shape, F32)
    vals = []
    for kk in range(TOP_K):
        m = jnp.max(logits, axis=-1, keepdims=True)
        idx = jnp.min(jnp.where(logits == m, lane_f, float(LANES)), axis=-1, keepdims=True)
        logits = jnp.where(lane_f == idx, 2.0 * NEG_BIG, logits)
        idx_out = jnp.where(lane == kk, idx, idx_out)
        vals.append(m)
    es = [jnp.exp(vv - vals[0]) for vv in vals]
    denom = es[0] + es[1] + es[2] + es[3]
    gate_out = jnp.zeros(logits.shape, F32)
    for kk in range(TOP_K):
        gate_out = jnp.where(lane == kk, es[kk] / denom, gate_out)
    idx_ref[...] = idx_out.astype(jnp.int32)
    gate_ref[...] = gate_out


def _out_proj(x_all, o_diff, o_gla, w_out_b, mod_tab, norm_w, rw_pad, rb_pad, l, n_ctx, dec_seq):
    n, d = x_all.shape
    tm = ROW_TILE
    mi = lambda c: _mod_index(l, c, tm, n_ctx, dec_seq)
    return pl.pallas_call(
        _out_proj_kernel,
        grid=(n // tm,),
        in_specs=[pl.BlockSpec((tm, d), lambda i: (i, 0)),
                  pl.BlockSpec((tm, DIFF_W), lambda i: (i, 0)),
                  pl.BlockSpec((tm, GLA_VW), lambda i: (i, 0)),
                  pl.BlockSpec((None, d, d), lambda i: (l, 0, 0)),
                  pl.BlockSpec((None, 1, d), mi(2)),
                  pl.BlockSpec((None, 1, d), lambda i: (l, 0, 0)),
                  pl.BlockSpec((None, 1, d), mi(3)),
                  pl.BlockSpec((None, 1, d), mi(4)),
                  pl.BlockSpec((None, d, LANES), lambda i: (l, 0, 0)),
                  pl.BlockSpec((None, 1, LANES), lambda i: (l, 0, 0))],
        out_specs=[pl.BlockSpec((tm, d), lambda i: (i, 0)),
                   pl.BlockSpec((tm, d), lambda i: (i, 0)),
                   pl.BlockSpec((tm, LANES), lambda i: (i, 0)),
                   pl.BlockSpec((tm, LANES), lambda i: (i, 0))],
        out_shape=[jax.ShapeDtypeStruct((n, d), F32),
                   jax.ShapeDtypeStruct((n, d), F32),
                   jax.ShapeDtypeStruct((n, LANES), jnp.int32),
                   jax.ShapeDtypeStruct((n, LANES), F32)],
        compiler_params=_cparams(("parallel",), 48),
        name="out_proj_router",
    )(x_all, o_diff, o_gla, w_out_b, mod_tab, norm_w, mod_tab, mod_tab, rw_pad, rb_pad)


def _row_copy(src, src_row, dst, dst_row, sem):
    return pltpu.make_async_copy(src.at[pl.ds(src_row, 1)], dst.at[pl.ds(dst_row, 1)], sem)


def _moe_kernel(items_ref, sbe_ref, sbs_ref, sbc_ref, nu_ref,
                h_hbm, wg_ref, wl_ref, wd_ref, bg_ref, bl_ref, bd_ref,
                y_hbm,
                acc, xb, wg_s, wl_s, wd_s, sem, *, n_tok, n_f):
    s = pl.program_id(0)
    f = pl.program_id(1)
    cnt = sbc_ref[s]
    start = sbs_ref[s]
    subs = [(r * MOE_SUB, slice(r * MOE_SUB, (r + 1) * MOE_SUB)) for r in range(MOE_ROWS // MOE_SUB)]

    @pl.when(jnp.logical_and(s == 0, f == 0))
    def _():
        acc[...] = jnp.zeros_like(acc)

    @pl.when(jnp.logical_and(cnt > 0, f == 0))
    def _():
        def issue(p, carry):
            tok = items_ref[start + p] // TOP_K
            _row_copy(h_hbm, tok, acc, p, sem).start()
            return carry
        lax.fori_loop(0, cnt, issue, 0)

        def wait(p, carry):
            _row_copy(h_hbm, 0, acc, p, sem).wait()
            return carry
        lax.fori_loop(0, cnt, wait, 0)
        for r0, rows in subs:
            @pl.when(r0 < cnt)
            def _():
                xb[rows, :] = acc[rows, :].astype(BF16)

    @pl.when(cnt > 0)
    def _():
        wg_s[...] = wg_ref[...].astype(BF16)
        wl_s[...] = wl_ref[...].astype(BF16)
        wd_s[...] = wd_ref[...].astype(BF16)
        for r0, rows in subs:
            @pl.when(r0 < cnt)
            def _():
                x = xb[rows, :]
                glu = jnp.minimum(_dot(x, wg_s[...]) + bg_ref[...], SWIGLU_LIMIT)
                lin = jnp.clip(_dot(x, wl_s[...]) + bl_ref[...], -SWIGLU_LIMIT, SWIGLU_LIMIT)
                act = glu * (1.0 / (1.0 + jnp.exp(-SWIGLU_ALPHA * glu))) * (lin + 1.0)
                contrib = _dot(act.astype(BF16), wd_s[...])

                @pl.when(f == 0)
                def _():
                    acc[rows, :] = contrib + bd_ref[...]

                @pl.when(f > 0)
                def _():
                    acc[rows, :] += contrib

    @pl.when(jnp.logical_and(cnt > 0, f == n_f - 1))
    def _():
        def issue(p, carry):
            item = items_ref[start + p]
            _row_copy(acc, p, y_hbm, (item % TOP_K) * n_tok + item // TOP_K, sem).start()
            return carry
        lax.fori_loop(0, cnt, issue, 0)

        def wait(p, carry):
            _row_copy(acc, p, y_hbm, 0, sem).wait()
            return carry
        lax.fori_loop(0, cnt, wait, 0)


def _experts(h, items, sb_e, sb_start, sb_cnt, n_used, w_gu, b_gu, w_dn, b_dn, l):
    n_tok, d = h.shape
    n_sb = sb_e.shape[0]
    tf = MOE_FF_TILE
    nf = D_FF // tf

    def ftile(s, f, nu):
        return jnp.where(s < nu[0], f, nf - 1)

    return pl.pallas_call(
        functools.partial(_moe_kernel, n_tok=n_tok, n_f=nf),
        grid_spec=pltpu.PrefetchScalarGridSpec(
            num_scalar_prefetch=5,
            grid=(n_sb, nf),
            in_specs=[pl.BlockSpec(memory_space=pl.ANY),
                      pl.BlockSpec((None, None, d, tf), lambda s, f, it, e, st, c, nu: (l, e[s], 0, ftile(s, f, nu))),
                      pl.BlockSpec((None, None, d, tf), lambda s, f, it, e, st, c, nu: (l, e[s], 0, nf + ftile(s, f, nu))),
                      pl.BlockSpec((None, None, tf, d), lambda s, f, it, e, st, c, nu: (l, e[s], ftile(s, f, nu), 0)),
                      pl.BlockSpec((None, None, 1, tf), lambda s, f, it, e, st, c, nu: (l, e[s], 0, ftile(s, f, nu))),
                      pl.BlockSpec((None, None, 1, tf), lambda s, f, it, e, st, c, nu: (l, e[s], 0, nf + ftile(s, f, nu))),
                      pl.BlockSpec((None, None, 1, d), lambda s, f, it, e, st, c, nu: (l, e[s], 0, 0))],
            out_specs=pl.BlockSpec(memory_space=pl.ANY),
            scratch_shapes=[pltpu.VMEM((MOE_ROWS, d), F32),
                            pltpu.VMEM((MOE_ROWS, d), BF16),
                            pltpu.VMEM((d, tf), BF16),
                            pltpu.VMEM((d, tf), BF16),
                            pltpu.VMEM((tf, d), BF16),
                            pltpu.SemaphoreType.DMA(())]),
        out_shape=jax.ShapeDtypeStruct((TOP_K * n_tok, d), F32),
        compiler_params=_cparams(("arbitrary", "arbitrary"), 56),
        name="moe_experts",
    )(items, sb_e, sb_start, sb_cnt, n_used, h, w_gu, w_gu, w_dn, b_gu, b_gu, b_dn)


def _route(top_idx):
    n_items = top_idx.shape[0] * TOP_K
    flat_e = top_idx.reshape(-1)
    items = jnp.argsort(flat_e).astype(jnp.int32)
    experts = jnp.arange(N_EXPERTS, dtype=jnp.int32)
    counts = jnp.sum((flat_e[:, None] == experts[None, :]).astype(jnp.int32), axis=0)
    starts = jnp.cumsum(counts) - counts
    groups = (counts + MOE_ROWS - 1) // MOE_ROWS
    group_end = jnp.cumsum(groups)
    n_used = group_end[-1]
    n_sb = n_items // MOE_ROWS + N_EXPERTS
    s = jnp.minimum(jnp.arange(n_sb, dtype=jnp.int32), n_used - 1)
    sb_e = jnp.sum((group_end[None, :] <= s[:, None]).astype(jnp.int32), axis=1)
    j = s - (group_end[sb_e] - groups[sb_e])
    sb_start = starts[sb_e] + j * MOE_ROWS
    live = jnp.arange(n_sb, dtype=jnp.int32) < n_used
    sb_cnt = jnp.where(live, jnp.clip(counts[sb_e] - j * MOE_ROWS, 0, MOE_ROWS), 0)
    i32 = lambda a: a.astype(jnp.int32)
    return items, i32(sb_e), i32(sb_start), i32(sb_cnt), i32(n_used).reshape(1)


def _combine_kernel(x_ref, y0_ref, y1_ref, y2_ref, y3_ref, g_ref, gf_ref, o_ref):
    g = g_ref[...]
    acc = y0_ref[...] * g[:, 0:1]
    for kk, y_ref in enumerate((y1_ref, y2_ref, y3_ref), start=1):
        acc = acc + y_ref[...] * g[:, kk:kk + 1]
    o_ref[...] = x_ref[...] + gf_ref[...] * acc


def _combine(x_all, y_items, gates, mod_tab, l, n_ctx, dec_seq):
    n, d = x_all.shape
    tm = ROW_TILE
    nt = n // tm
    slab = lambda kk: pl.BlockSpec((tm, d), lambda i: (kk * nt + i, 0))
    return pl.pallas_call(
        _combine_kernel,
        grid=(nt,),
        in_specs=[pl.BlockSpec((tm, d), lambda i: (i, 0)),
                  slab(0), slab(1), slab(2), slab(3),
                  pl.BlockSpec((tm, LANES), lambda i: (i, 0)),
                  pl.BlockSpec((None, 1, d), _mod_index(l, 5, tm, n_ctx, dec_seq))],
        out_specs=pl.BlockSpec((tm, d), lambda i: (i, 0)),
        out_shape=jax.ShapeDtypeStruct((n, d), F32),
        compiler_params=_cparams(("parallel",), 48),
        name="moe_combine",
    )(x_all, y_items, y_items, y_items, y_items, gates, mod_tab)


def kernel(x_prompt, x_sample, cache_diff_k, cache_diff_v, state_gla, c, c_ctx, norm_attn_w, norm_ffn_w, w_ada, b_ada, w_in, q_norm_w, k_norm_w, diff_lambda_w, diff_out_norm_w, w_gla_a2, b_gla_a, gla_out_norm_w, w_out, router_w, router_b, w_gate_up, b_gate_up, w_down, b_down):
    batch, seq, d = x_prompt.shape
    dec_batch, dec_seq, _ = x_sample.shape
    depth = w_in.shape[0]
    past = cache_diff_k.shape[2]
    n_ctx = batch * seq
    n_lat = dec_batch * dec_seq
    assert d == D_MODEL and 1 + dec_batch <= 8
    assert seq % ROW_TILE == 0 and dec_seq % ROW_TILE == 0 and past % ROW_TILE == 0
    tm_proj = 512 if (n_ctx % 512 == 0 and dec_seq % 512 == 0) else ROW_TILE

    cond8 = jnp.zeros((8, d), F32).at[0].set(c_ctx).at[1:1 + dec_batch].set(c)
    w_main = w_in[:, :, :PROJ_W].astype(BF16)
    w_ga = w_in[:, :, PROJ_W:].reshape(depth, d, 2, GLA_GATE_RANK).transpose(0, 2, 1, 3)
    w_ga = jnp.pad(w_ga, ((0, 0), (0, 0), (0, 0), (0, LANES - GLA_GATE_RANK))).astype(BF16)
    a2_pad = jnp.pad(w_gla_a2, ((0, 0), (0, 0), (0, LANES - GLA_GATE_RANK), (0, 0)))
    b_a = b_gla_a.reshape(depth, 2, 1, GLA_KW)
    qw = jnp.tile(q_norm_w, (1, 2)).reshape(depth, 1, HEAD_W)
    kw = jnp.tile(k_norm_w, (1, 2)).reshape(depth, 1, HEAD_W)
    w_out_b = w_out.astype(BF16)
    rw_pad = jnp.pad(router_w, ((0, 0), (0, 0), (0, LANES - N_EXPERTS))).astype(BF16)
    rb_pad = jnp.pad(router_b, ((0, 0), (0, LANES - N_EXPERTS)), constant_values=NEG_BIG).reshape(depth, 1, LANES)
    b_gu = b_gate_up.reshape(depth, N_EXPERTS, 1, 2 * D_FF)
    b_dn = b_down.reshape(depth, N_EXPERTS, 1, d)
    cos_t, sin_t = _rope_tables(dec_seq)
    cache_k = cache_diff_k.reshape(dec_batch, depth, past, DIFF_W)
    cache_v = cache_diff_v.reshape(dec_batch, depth, past, DIFF_W)

    mod_tab = _adaln(cond8, w_ada, b_ada).reshape(depth * 8 * 6, 1, d)
    x_all = jnp.concatenate([x_prompt.reshape(n_ctx, d), x_sample.reshape(n_lat, d)], axis=0)
    zero_state = jnp.zeros((batch, N_GLA_HEADS, GLA_DK, GLA_DV), F32)

    ks, vs, sts = [], [], []
    for l in range(depth):
        lam_init = 0.8 - 0.6 * math.exp(-0.3 * l)
        proj, la = _in_proj(x_all, norm_attn_w.reshape(depth, 1, d), mod_tab, w_main, w_ga, a2_pad, b_a,
                            l, n_ctx, dec_seq, tm_proj)
        qh, kt, vh, kc = _prep_ctx(proj, qw[l], kw[l], batch, seq)
        od_ctx = _attention(qh, kt, vh, diff_lambda_w[l], diff_out_norm_w[l].reshape(1, HEAD_W),
                            lam_init, ROW_TILE)
        og_ctx, s_f, s_b = _gla(proj, la, zero_state, zero_state, gla_out_norm_w[l].reshape(1, GLA_DV),
                                0, batch, seq, True)
        ks.append(kc.reshape(batch, seq, N_DIFF_HEADS, 2, DIFF_HEAD_DIM))
        vs.append(proj[:n_ctx, 2 * DIFF_W:3 * DIFF_W].reshape(batch, seq, N_DIFF_HEADS, HEAD_W))
        sts.append(jnp.stack([s_f, s_b], axis=1))
        qh, kt, vh = _prep_lat(proj, cache_k, cache_v, qw[l], kw[l], cos_t, sin_t, l, n_ctx,
                               dec_batch, dec_seq, past)
        od_lat = _attention(qh, kt, vh, diff_lambda_w[l], diff_out_norm_w[l].reshape(1, HEAD_W),
                            lam_init, ROW_TILE)
        og_lat, _, _ = _gla(proj, la, state_gla[:, l, 0], state_gla[:, l, 1],
                            gla_out_norm_w[l].reshape(1, GLA_DV), n_ctx, dec_batch, dec_seq, False)
        o_diff = jnp.concatenate([od_ctx.reshape(n_ctx, DIFF_W), od_lat.reshape(n_lat, DIFF_W)], axis=0)
        o_gla = jnp.concatenate([og_ctx, og_lat], axis=0)
        x_all, h2, top_idx, gates = _out_proj(x_all, o_diff, o_gla, w_out_b, mod_tab,
                                              norm_ffn_w.reshape(depth, 1, d), rw_pad, rb_pad,
                                              l, n_ctx, dec_seq)
        items, sb_e, sb_start, sb_cnt, n_used = _route(top_idx[:, :TOP_K])
        y_items = _experts(h2, items, sb_e, sb_start, sb_cnt, n_used, w_gate_up, b_gu, w_down, b_dn, l)
        x_all = _combine(x_all, y_items, gates, mod_tab, l, n_ctx, dec_seq)

    y_prompt = x_all[:n_ctx].reshape(batch, seq, d)
    y_sample = x_all[n_ctx:].reshape(dec_batch, dec_seq, d)
    return (y_prompt, y_sample, jnp.stack(ks, axis=1), jnp.stack(vs, axis=1), jnp.stack(sts, axis=1))
```

```python
import functools
import math

import jax
import jax.numpy as jnp
from jax import lax
from jax.experimental import pallas as pl
from jax.experimental.pallas import tpu as pltpu

F32 = jnp.float32
BF16 = jnp.bfloat16

D_MODEL = 2048
GRID_W = 64
N_DIFF_HEADS = 8
DIFF_HEAD_DIM = 64
HEAD_W = 2 * DIFF_HEAD_DIM
ROPE_AXIS_DIM = DIFF_HEAD_DIM // 2
ROPE_BASE = 10000.0
N_GLA_HEADS = 4
GLA_DK = 128
GLA_DV = 256
GLA_GATE_RANK = 16
GLA_GATE_TAU = 16.0
GLA_CHUNK = 64
N_EXPERTS = 32
TOP_K = 4
D_FF = D_MODEL
SWIGLU_LIMIT = 7.0
SWIGLU_ALPHA = 1.702
EPS = 1e-6

DIFF_W = N_DIFF_HEADS * HEAD_W
GLA_KW = N_GLA_HEADS * GLA_DK
GLA_VW = N_GLA_HEADS * GLA_DV
PROJ_W = 3 * DIFF_W + 2 * GLA_KW + 2 * GLA_VW
LANES = 128
ROW_TILE = 256
MOE_ROWS = 1024
MOE_SUB = 512
MOE_FF_TILE = 512
DMA_UNROLL_LOG2 = 3
DMA_UNROLL = 1 << DMA_UNROLL_LOG2
TOP_K_LOG2 = 2
NEG_BIG = -1e30
Q_SCALE = DIFF_HEAD_DIM ** -0.5 * math.log2(math.e)


def _cparams(sem, vmem_mb):
    return pltpu.CompilerParams(dimension_semantics=sem, vmem_limit_bytes=vmem_mb * 2**20)


def _dot(a, b):
    return jnp.dot(a, b, preferred_element_type=F32)


def _split_dot(a, b, dims=None):
    hi = a.astype(BF16)
    lo = (a - hi.astype(F32)).astype(BF16)
    if dims is None:
        return _dot(hi, b) + _dot(lo, b)
    return (lax.dot_general(hi, b, dims, preferred_element_type=F32)
            + lax.dot_general(lo, b, dims, preferred_element_type=F32))


def _adaln_kernel(c_ref, w_ref, b_ref, o_ref):
    c = c_ref[...]
    s = c * (1.0 / (1.0 + jnp.exp(-c)))
    o_ref[...] = _dot(s.astype(BF16), w_ref[...].astype(BF16)) + b_ref[...]


def _adaln(cond8, w_ada, b_ada):
    depth, d, n = w_ada.shape
    tn = 1024
    return pl.pallas_call(
        _adaln_kernel,
        grid=(depth, n // tn),
        in_specs=[pl.BlockSpec((8, d), lambda l, j: (0, 0)),
                  pl.BlockSpec((None, d, tn), lambda l, j: (l, 0, j)),
                  pl.BlockSpec((None, 1, tn), lambda l, j: (l, 0, j))],
        out_specs=pl.BlockSpec((None, 8, tn), lambda l, j: (l, 0, j)),
        out_shape=jax.ShapeDtypeStruct((depth, 8, n), F32),
        compiler_params=_cparams(("parallel", "parallel"), 40),
        name="adaln",
    )(cond8, w_ada, b_ada.reshape(depth, 1, n))


def _mod_index(l, chunk, tm, n_ctx, dec_seq):
    def index_map(i, *_):
        r = jnp.where(i * tm < n_ctx, 0, 1 + (i * tm - n_ctx) // dec_seq)
        return ((l * 8 + r) * 6 + chunk, 0, 0)
    return index_map


def _in_proj_kernel(x_ref, nw_ref, sh_ref, sc_ref, w_ref, wga_ref, a2_ref, ba_ref,
                    o_ref, la_ref, h_scr):
    @pl.when(pl.program_id(1) == 0)
    def _():
        x = x_ref[...]
        y = x * lax.rsqrt(jnp.mean(x * x, axis=-1, keepdims=True) + EPS) * nw_ref[...]
        hb = (y * (1.0 + sc_ref[...]) + sh_ref[...]).astype(BF16)
        h_scr[...] = hb
        for m in range(2):
            ga = _dot(hb, wga_ref[m])
            z = _dot(ga.astype(BF16), a2_ref[m].astype(BF16)) + ba_ref[m]
            ls = jnp.minimum(z, 0.0) - jnp.log1p(jnp.exp(-jnp.abs(z)))
            la_ref[:, m * GLA_KW:(m + 1) * GLA_KW] = ls * (1.0 / GLA_GATE_TAU)

    o_ref[...] = _dot(h_scr[...], w_ref[...])


def _in_proj(x_all, norm_w, mod_tab, w_main, w_ga, a2_pad, b_a, l, n_ctx, dec_seq, tm):
    n, d = x_all.shape
    tn = 1024
    return pl.pallas_call(
        _in_proj_kernel,
        grid=(n // tm, PROJ_W // tn),
        in_specs=[pl.BlockSpec((tm, d), lambda i, j: (i, 0)),
                  pl.BlockSpec((None, 1, d), lambda i, j: (l, 0, 0)),
                  pl.BlockSpec((None, 1, d), _mod_index(l, 0, tm, n_ctx, dec_seq)),
                  pl.BlockSpec((None, 1, d), _mod_index(l, 1, tm, n_ctx, dec_seq)),
                  pl.BlockSpec((None, d, tn), lambda i, j: (l, 0, j)),
                  pl.BlockSpec((None, 2, d, LANES), lambda i, j: (l, 0, 0, 0)),
                  pl.BlockSpec((None, 2, LANES, GLA_KW), lambda i, j: (l, 0, 0, 0)),
                  pl.BlockSpec((None, 2, 1, GLA_KW), lambda i, j: (l, 0, 0, 0))],
        out_specs=[pl.BlockSpec((tm, tn), lambda i, j: (i, j)),
                   pl.BlockSpec((tm, 2 * GLA_KW), lambda i, j: (i, 0))],
        out_shape=[jax.ShapeDtypeStruct((n, PROJ_W), F32),
                   jax.ShapeDtypeStruct((n, 2 * GLA_KW), F32)],
        scratch_shapes=[pltpu.VMEM((tm, d), BF16)],
        compiler_params=_cparams(("parallel", "arbitrary"), 48),
        name="in_proj",
    )(x_all, norm_w, mod_tab, mod_tab, w_main, w_ga, a2_pad, b_a)


def _group_ones():
    r = lax.broadcasted_iota(jnp.int32, (HEAD_W, HEAD_W), 0) // DIFF_HEAD_DIM
    c = lax.broadcasted_iota(jnp.int32, (HEAD_W, HEAD_W), 1) // DIFF_HEAD_DIM
    return (r == c).astype(BF16)


def _head_norm(x, w, g):
    ss = _split_dot(x * x, g)
    return x * lax.rsqrt(ss * (1.0 / DIFF_HEAD_DIM) + EPS) * w


def _rope(y, cos, sin_signed, lo_mask):
    half = ROPE_AXIS_DIM // 2
    partner = jnp.where(lo_mask, pltpu.roll(y, HEAD_W - half, 1), pltpu.roll(y, half, 1))
    return y * cos + partner * sin_signed


def _store_v(vh_ref, v_ref):
    ones = jnp.ones((v_ref.shape[0], HEAD_W), BF16)
    for h in range(N_DIFF_HEADS):
        vh_ref[:, 2 * h * HEAD_W:(2 * h + 1) * HEAD_W] = v_ref[:, h * HEAD_W:(h + 1) * HEAD_W].astype(BF16)
        vh_ref[:, (2 * h + 1) * HEAD_W:(2 * h + 2) * HEAD_W] = ones


def _prep_ctx_kernel(q_ref, k_ref, v_ref, qw_ref, kw_ref, qh_ref, kt_ref, vh_ref, kc_ref):
    g = _group_ones()
    for h in range(N_DIFF_HEADS):
        sl = slice(h * HEAD_W, (h + 1) * HEAD_W)
        qn = _head_norm(q_ref[:, sl], qw_ref[...], g)
        kn = _head_norm(k_ref[:, sl], kw_ref[...], g)
        qh_ref[:, sl] = (qn * Q_SCALE).astype(BF16)
        kc_ref[:, sl] = kn
        kt_ref[h] = kn.T.astype(BF16)
    _store_v(vh_ref, v_ref)


def _prep_ctx(proj, qw, kw, batch, seq):
    tm = ROW_TILE
    nt = seq // tm
    row = lambda c: (lambda b, j: (b * nt + j, c))
    return pl.pallas_call(
        _prep_ctx_kernel,
        grid=(batch, nt),
        in_specs=[pl.BlockSpec((tm, DIFF_W), row(0)),
                  pl.BlockSpec((tm, DIFF_W), row(1)),
                  pl.BlockSpec((tm, DIFF_W), row(2)),
                  pl.BlockSpec((1, HEAD_W), lambda b, j: (0, 0)),
                  pl.BlockSpec((1, HEAD_W), lambda b, j: (0, 0))],
        out_specs=[pl.BlockSpec((None, tm, DIFF_W), lambda b, j: (b, j, 0)),
                   pl.BlockSpec((None, N_DIFF_HEADS, HEAD_W, tm), lambda b, j: (b, 0, 0, j)),
                   pl.BlockSpec((None, tm, 2 * DIFF_W), lambda b, j: (b, j, 0)),
                   pl.BlockSpec((None, tm, DIFF_W), lambda b, j: (b, j, 0))],
        out_shape=[jax.ShapeDtypeStruct((batch, seq, DIFF_W), BF16),
                   jax.ShapeDtypeStruct((batch, N_DIFF_HEADS, HEAD_W, seq), BF16),
                   jax.ShapeDtypeStruct((batch, seq, 2 * DIFF_W), BF16),
                   jax.ShapeDtypeStruct((batch, seq, DIFF_W), F32)],
        compiler_params=_cparams(("parallel", "parallel"), 40),
        name="prep_ctx",
    )(proj, proj, proj, qw, kw)


def _prep_lat_kernel(q_ref, k_ref, v_ref, ck_ref, cv_ref, qw_ref, kw_ref, cos_ref, sin_ref,
                     qh_ref, kt_ref, vh_ref, *, n_lat_tiles):
    j = pl.program_id(1)

    @pl.when(j < n_lat_tiles)
    def _():
        g = _group_ones()
        cos = cos_ref[...]
        sin = sin_ref[...]
        lane = lax.broadcasted_iota(jnp.int32, cos.shape, 1)
        lo_mask = (lane % ROPE_AXIS_DIM) < (ROPE_AXIS_DIM // 2)
        for h in range(N_DIFF_HEADS):
            sl = slice(h * HEAD_W, (h + 1) * HEAD_W)
            qn = _rope(_head_norm(q_ref[:, sl], qw_ref[...], g), cos, sin, lo_mask)
            kn = _rope(_head_norm(k_ref[:, sl], kw_ref[...], g), cos, sin, lo_mask)
            qh_ref[:, sl] = (qn * Q_SCALE).astype(BF16)
            kt_ref[h] = kn.T.astype(BF16)
        _store_v(vh_ref, v_ref)

    @pl.when(j >= n_lat_tiles)
    def _():
        for h in range(N_DIFF_HEADS):
            kt_ref[h] = ck_ref[:, h * HEAD_W:(h + 1) * HEAD_W].T.astype(BF16)
        _store_v(vh_ref, cv_ref)


def _prep_lat(proj, cache_k, cache_v, qw, kw, cos_t, sin_t, l, n_ctx, dec_batch, dec_seq, past):
    tm = ROW_TILE
    nl = dec_seq // tm
    npast = past // tm
    row0 = n_ctx // tm
    row = lambda c: (lambda b, j: (row0 + b * nl + jnp.minimum(j, nl - 1), c))
    cache = lambda b, j: (b, l, jnp.maximum(j - nl, 0), 0)
    tab = lambda b, j: (jnp.minimum(j, nl - 1), 0)
    return pl.pallas_call(
        functools.partial(_prep_lat_kernel, n_lat_tiles=nl),
        grid=(dec_batch, nl + npast),
        in_specs=[pl.BlockSpec((tm, DIFF_W), row(0)),
                  pl.BlockSpec((tm, DIFF_W), row(1)),
                  pl.BlockSpec((tm, DIFF_W), row(2)),
                  pl.BlockSpec((None, None, tm, DIFF_W), cache),
                  pl.BlockSpec((None, None, tm, DIFF_W), cache),
                  pl.BlockSpec((1, HEAD_W), lambda b, j: (0, 0)),
                  pl.BlockSpec((1, HEAD_W), lambda b, j: (0, 0)),
                  pl.BlockSpec((tm, HEAD_W), tab),
                  pl.BlockSpec((tm, HEAD_W), tab)],
        out_specs=[pl.BlockSpec((None, tm, DIFF_W), lambda b, j: (b, jnp.minimum(j, nl - 1), 0)),
                   pl.BlockSpec((None, N_DIFF_HEADS, HEAD_W, tm), lambda b, j: (b, 0, 0, j)),
                   pl.BlockSpec((None, tm, 2 * DIFF_W), lambda b, j: (b, j, 0))],
        out_shape=[jax.ShapeDtypeStruct((dec_batch, dec_seq, DIFF_W), BF16),
                   jax.ShapeDtypeStruct((dec_batch, N_DIFF_HEADS, HEAD_W, dec_seq + past), BF16),
                   jax.ShapeDtypeStruct((dec_batch, dec_seq + past, 2 * DIFF_W), BF16)],
        compiler_params=_cparams(("parallel", "arbitrary"), 40),
        name="prep_lat",
    )(proj, proj, proj, cache_k, cache_v, qw, kw, cos_t, sin_t)


def _rope_tables(n_tok):
    t = jnp.arange(n_tok)
    r = (t // GRID_W).astype(F32)
    c = (t % GRID_W).astype(F32)
    inv = ROPE_BASE ** (-jnp.arange(0, ROPE_AXIS_DIM, 2, dtype=F32) / ROPE_AXIS_DIM)
    ar = r[:, None] * inv
    ac = c[:, None] * inv
    cos64 = jnp.concatenate([jnp.cos(ar), jnp.cos(ar), jnp.cos(ac), jnp.cos(ac)], axis=-1)
    sin64 = jnp.concatenate([-jnp.sin(ar), jnp.sin(ar), -jnp.sin(ac), jnp.sin(ac)], axis=-1)
    return jnp.tile(cos64, (1, 2)), jnp.tile(sin64, (1, 2))


def _attn_kernel(q_ref, kt_ref, v_ref, lw_ref, ow_ref, o_ref, *, lam_init):
    q = q_ref[...]
    lane = lax.broadcasted_iota(jnp.int32, q.shape, 1)
    kt = kt_ref[...]
    v = v_ref[...]

    def attend(qm):
        s = _dot(qm, kt)
        p = jnp.exp2(s - jnp.max(s, axis=-1, keepdims=True))
        pv = _dot(p.astype(BF16), v)
        return pv[:, :HEAD_W] / pv[:, HEAD_W:HEAD_W + 1]

    o1 = attend(jnp.where(lane < DIFF_HEAD_DIM, q, jnp.zeros_like(q)))
    o2 = attend(jnp.where(lane >= DIFF_HEAD_DIM, q, jnp.zeros_like(q)))
    lw = lw_ref[...]
    lam = (jnp.exp(jnp.sum(lw[0:1] * lw[1:2], axis=-1, keepdims=True))
           - jnp.exp(jnp.sum(lw[2:3] * lw[3:4], axis=-1, keepdims=True)) + lam_init)
    o = o1 - lam * o2
    y = o * lax.rsqrt(jnp.mean(o * o, axis=-1, keepdims=True) + EPS) * ow_ref[...]
    o_ref[...] = (y * (1.0 - lam_init)).astype(BF16)


def _attention(qh, kt, vh, lam_w, out_w, lam_init, tq):
    batch, tq_total, _ = qh.shape
    tk = kt.shape[-1]
    return pl.pallas_call(
        functools.partial(_attn_kernel, lam_init=lam_init),
        grid=(batch, N_DIFF_HEADS, tq_total // tq),
        in_specs=[pl.BlockSpec((None, tq, HEAD_W), lambda b, h, i: (b, i, h)),
                  pl.BlockSpec((None, None, HEAD_W, tk), lambda b, h, i: (b, h, 0, 0)),
                  pl.BlockSpec((None, tk, 2 * HEAD_W), lambda b, h, i: (b, 0, h)),
                  pl.BlockSpec((4, DIFF_HEAD_DIM), lambda b, h, i: (0, 0)),
                  pl.BlockSpec((1, HEAD_W), lambda b, h, i: (0, 0))],
        out_specs=pl.BlockSpec((None, tq, HEAD_W), lambda b, h, i: (b, i, h)),
        out_shape=jax.ShapeDtypeStruct((batch, tq_total, DIFF_W), BF16),
        compiler_params=_cparams(("parallel", "parallel", "parallel"), 48),
        name="diff_attn",
    )(qh, kt, vh, lam_w, out_w)


def _gla_kernel(*refs, reverse, final, zero_init, n_chunks):
    if final:
        q_ref, k_ref, v_ref, la_ref, s0_ref, of_ref, gg_ref, gw_ref, o_ref, s_ref, state = refs
    else:
        q_ref, k_ref, v_ref, la_ref, s0_ref, o_ref, s_ref, state = refs
    t = pl.program_id(2)

    @pl.when(t == 0)
    def _():
        state[...] = jnp.zeros_like(state) if zero_init else s0_ref[...]

    r = lax.broadcasted_iota(jnp.int32, (GLA_CHUNK, GLA_CHUNK), 0)
    c = lax.broadcasted_iota(jnp.int32, (GLA_CHUNK, GLA_CHUNK), 1)
    tri = (c >= r) if reverse else (c <= r)
    tri_b = tri.astype(BF16)
    ones_b = jnp.ones((GLA_CHUNK, GLA_DK), BF16)
    contract_rows = (((0,), (0,)), ((), ()))
    chunks = range(n_chunks - 1, -1, -1) if reverse else range(n_chunks)
    for ci in chunks:
        rows = slice(ci * GLA_CHUNK, (ci + 1) * GLA_CHUNK)
        la = la_ref[rows, :]
        hi = la.astype(BF16)
        lo = (la - hi.astype(F32)).astype(BF16)
        b = _dot(tri_b, hi) + _dot(tri_b, lo)
        b_last = b[0:1] if reverse else b[GLA_CHUNK - 1:GLA_CHUNK]
        q = q_ref[rows, :] * GLA_DK ** -0.5
        k = k_ref[rows, :]
        v = v_ref[rows, :].astype(BF16)
        q_dec = (q * jnp.exp(b)).astype(BF16)
        k_inv = (k * jnp.exp(-b)).astype(BF16)
        k_end = (k * jnp.exp(b_last - b)).astype(BF16)
        att = lax.dot_general(q_dec, k_inv, (((1,), (1,)), ((), ())), preferred_element_type=F32)
        att = jnp.where(tri, att, 0.0).astype(BF16)
        s_prev = state[...]
        o = _dot(att, v) + _dot(q_dec, s_prev.astype(BF16))
        u = lax.dot_general(k_end, v, contract_rows, preferred_element_type=F32)
        tot = (lax.dot_general(hi, ones_b, contract_rows, preferred_element_type=F32)
               + lax.dot_general(lo, ones_b, contract_rows, preferred_element_type=F32))
        decay = jnp.exp(tot)
        state[...] = jnp.concatenate([decay, decay], axis=1) * s_prev + u
        if final:
            tot_o = o + of_ref[rows, :]
            y = tot_o * lax.rsqrt(jnp.mean(tot_o * tot_o, axis=-1, keepdims=True) + EPS) * gw_ref[...]
            gg = gg_ref[rows, :]
            y = y * (gg * (1.0 / (1.0 + jnp.exp(-gg))))
            o_ref[rows, :] = y.astype(BF16)
        else:
            o_ref[rows, :] = o

    @pl.when(t == pl.num_programs(2) - 1)
    def _():
        s_ref[...] = state[...]


def _gla_pass(proj, la, s0, o_fwd, gla_w, row0, batch, seq, direction, zero_init):
    tb = ROW_TILE
    nt = seq // tb
    reverse = direction == 1
    final = o_fwd is not None
    tt = (lambda t: nt - 1 - t) if reverse else (lambda t: t)
    rb0 = row0 // tb
    q_c0 = 3 * DIFF_W // GLA_DK
    k_c0 = (3 * DIFF_W + GLA_KW) // GLA_DK
    v_c0 = (3 * DIFF_W + 2 * GLA_KW) // GLA_DV
    g_c0 = (3 * DIFF_W + 2 * GLA_KW + GLA_VW) // GLA_DV
    la_c0 = direction * N_GLA_HEADS
    prow = lambda c0: (lambda b, h, t: (rb0 + b * nt + tt(t), c0 + h))
    lrow = lambda c0: (lambda b, h, t: (b * nt + tt(t), c0 + h))
    in_specs = [pl.BlockSpec((tb, GLA_DK), prow(q_c0)),
                pl.BlockSpec((tb, GLA_DK), prow(k_c0)),
                pl.BlockSpec((tb, GLA_DV), prow(v_c0)),
                pl.BlockSpec((tb, GLA_DK), prow(la_c0)),
                pl.BlockSpec((None, None, GLA_DK, GLA_DV), lambda b, h, t: (b, h, 0, 0))]
    args = [proj, proj, proj, la, s0]
    if final:
        in_specs += [pl.BlockSpec((tb, GLA_DV), lrow(0)),
                     pl.BlockSpec((tb, GLA_DV), prow(g_c0)),
                     pl.BlockSpec((1, GLA_DV), lambda b, h, t: (0, 0))]
        args += [o_fwd, proj, gla_w]
    n_rows = batch * seq
    return pl.pallas_call(
        functools.partial(_gla_kernel, reverse=reverse, final=final, zero_init=zero_init,
                          n_chunks=tb // GLA_CHUNK),
        grid=(batch, N_GLA_HEADS, nt),
        in_specs=in_specs,
        out_specs=[pl.BlockSpec((tb, GLA_DV), lrow(0)),
                   pl.BlockSpec((None, None, GLA_DK, GLA_DV), lambda b, h, t: (b, h, 0, 0))],
        out_shape=[jax.ShapeDtypeStruct((n_rows, GLA_VW), BF16 if final else F32),
                   jax.ShapeDtypeStruct((batch, N_GLA_HEADS, GLA_DK, GLA_DV), F32)],
        scratch_shapes=[pltpu.VMEM((GLA_DK, GLA_DV), F32)],
        compiler_params=_cparams(("parallel", "parallel", "arbitrary"), 40),
        name="gla_bwd" if reverse else "gla_fwd",
    )(*args)


def _gla(proj, la, s0_f, s0_b, gla_w, row0, batch, seq, zero_init):
    o_f, s_f = _gla_pass(proj, la, s0_f, None, None, row0, batch, seq, 0, zero_init)
    o, s_b = _gla_pass(proj, la, s0_b, o_f, gla_w, row0, batch, seq, 1, zero_init)
    return o, s_f, s_b


def _out_proj_kernel(x_ref, od_ref, og_ref, w_ref, ga_ref, nw_ref, sh_ref, sc_ref, rw_ref, rb_ref,
                     xo_ref, h_ref, idx_ref, gate_ref):
    mixed = _dot(od_ref[...], w_ref[0:DIFF_W, :]) + _dot(og_ref[...], w_ref[DIFF_W:, :])
    x = x_ref[...] + ga_ref[...] * mixed
    xo_ref[...] = x
    y = x * lax.rsqrt(jnp.mean(x * x, axis=-1, keepdims=True) + EPS) * nw_ref[...]
    h = y * (1.0 + sc_ref[...]) + sh_ref[...]
    h_ref[...] = h
    logits = _dot(h.astype(BF16), rw_ref[...]) + rb_ref[...]
    lane = lax.broadcasted_iota(jnp.int32, logits.shape, 1)
    lane_f = lane.astype(F32)
    idx_out = jnp.zeros(logits.shape, F32)
    vals = []
    for kk in range(TOP_K):
        m = jnp.max(logits, axis=-1, keepdims=True)
        idx = jnp.min(jnp.where(logits == m, lane_f, float(LANES)), axis=-1, keepdims=True)
        logits = jnp.where(lane_f == idx, 2.0 * NEG_BIG, logits)
        idx_out = jnp.where(lane == kk, idx, idx_out)
        vals.append(m)
    es = [jnp.exp(vv - vals[0]) for vv in vals]
    denom = es[0] + es[1] + es[2] + es[3]
    gate_out = jnp.zeros(logits.shape, F32)
    for kk in range(TOP_K):
        gate_out = jnp.where(lane == kk, es[kk] / denom, gate_out)
    idx_ref[...] = idx_out.astype(jnp.int32)
    gate_ref[...] = gate_out


def _out_proj(x_all, o_diff, o_gla, w_out_b, mod_tab, norm_w, rw_pad, rb_pad, l, n_ctx, dec_seq):
    n, d = x_all.shape
    tm = ROW_TILE
    mi = lambda c: _mod_index(l, c, tm, n_ctx, dec_seq)
    return pl.pallas_call(
        _out_proj_kernel,
        grid=(n // tm,),
        in_specs=[pl.BlockSpec((tm, d), lambda i: (i, 0)),
                  pl.BlockSpec((tm, DIFF_W), lambda i: (i, 0)),
                  pl.BlockSpec((tm, GLA_VW), lambda i: (i, 0)),
                  pl.BlockSpec((None, d, d), lambda i: (l, 0, 0)),
                  pl.BlockSpec((None, 1, d), mi(2)),
                  pl.BlockSpec((None, 1, d), lambda i: (l, 0, 0)),
                  pl.BlockSpec((None, 1, d), mi(3)),
                  pl.BlockSpec((None, 1, d), mi(4)),
                  pl.BlockSpec((None, d, LANES), lambda i: (l, 0, 0)),
                  pl.BlockSpec((None, 1, LANES), lambda i: (l, 0, 0))],
        out_specs=[pl.BlockSpec((tm, d), lambda i: (i, 0)),
                   pl.BlockSpec((tm, d), lambda i: (i, 0)),
                   pl.BlockSpec((tm, LANES), lambda i: (i, 0)),
                   pl.BlockSpec((tm, LANES), lambda i: (i, 0))],
        out_shape=[jax.ShapeDtypeStruct((n, d), F32),
                   jax.ShapeDtypeStruct((n, d), F32),
                   jax.ShapeDtypeStruct((n, LANES), jnp.int32),
                   jax.ShapeDtypeStruct((n, LANES), F32)],
        compiler_params=_cparams(("parallel",), 48),
        name="out_proj_router",
    )(x_all, o_diff, o_gla, w_out_b, mod_tab, norm_w, mod_tab, mod_tab, rw_pad, rb_pad)


def _row_copy(src, src_row, dst, dst_row, sem):
    return pltpu.make_async_copy(src.at[pl.ds(src_row, 1)], dst.at[pl.ds(dst_row, 1)], sem)


def _for_rows(cnt, body):
    n_full = lax.shift_right_logical(cnt, DMA_UNROLL_LOG2)

    def chunk(i, carry):
        for u in range(DMA_UNROLL):
            body(i * DMA_UNROLL + u)
        return carry
    lax.fori_loop(0, n_full, chunk, 0)

    def tail(p, carry):
        body(p)
        return carry
    lax.fori_loop(n_full * DMA_UNROLL, cnt, tail, 0)


def _item_token(item):
    return lax.shift_right_logical(item, TOP_K_LOG2)


def _item_k(item):
    return lax.bitwise_and(item, TOP_K - 1)


def _moe_kernel(items_ref, sbe_ref, sbs_ref, sbc_ref, nu_ref,
                h_hbm, wg_ref, wl_ref, wd_ref, bg_ref, bl_ref, bd_ref,
                y_hbm,
                acc, xb, wg_s, wl_s, wd_s, sem, *, n_tok, n_f):
    s = pl.program_id(0)
    f = pl.program_id(1)
    cnt = sbc_ref[s]
    start = sbs_ref[s]
    subs = [r * MOE_SUB for r in range(MOE_ROWS // MOE_SUB)]

    @pl.when(jnp.logical_and(s == 0, f == 0))
    def _():
        acc[...] = jnp.zeros_like(acc)

    @pl.when(jnp.logical_and(cnt > 0, f == 0))
    def _():
        _for_rows(cnt, lambda p: _row_copy(h_hbm, _item_token(items_ref[start + p]), acc, p, sem).start())
        _for_rows(cnt, lambda p: _row_copy(h_hbm, 0, acc, p, sem).wait())
        for r0 in subs:
            @pl.when(r0 < cnt)
            def _():
                xb[r0:r0 + MOE_SUB, :] = acc[r0:r0 + MOE_SUB, :].astype(BF16)

    def sub_block(r0):
        rows = slice(r0, r0 + MOE_SUB)
        x = xb[rows, :]
        glu = jnp.minimum(_dot(x, wg_s[...]) + bg_ref[...], SWIGLU_LIMIT)
        lin = jnp.clip(_dot(x, wl_s[...]) + bl_ref[...], -SWIGLU_LIMIT, SWIGLU_LIMIT)
        act = glu * (1.0 / (1.0 + jnp.exp(-SWIGLU_ALPHA * glu))) * (lin + 1.0)
        contrib = _dot(act.astype(BF16), wd_s[...])
        acc[rows, :] = jnp.where(f == 0, bd_ref[...], acc[rows, :]) + contrib

    @pl.when(cnt > 0)
    def _():
        wg_s[...] = wg_ref[...].astype(BF16)
        wl_s[...] = wl_ref[...].astype(BF16)
        wd_s[...] = wd_ref[...].astype(BF16)
        sub_block(subs[0])
        for r0 in subs[1:]:
            pl.when(r0 < cnt)(functools.partial(sub_block, r0))

    @pl.when(jnp.logical_and(cnt > 0, f == n_f - 1))
    def _():
        def scatter(p):
            item = items_ref[start + p]
            _row_copy(acc, p, y_hbm, _item_k(item) * n_tok + _item_token(item), sem).start()
        _for_rows(cnt, scatter)
        _for_rows(cnt, lambda p: _row_copy(acc, p, y_hbm, 0, sem).wait())


def _experts(h, items, sb_e, sb_start, sb_cnt, n_used, w_gu, b_gu, w_dn, b_dn, l):
    n_tok, d = h.shape
    n_sb = sb_e.shape[0]
    tf = MOE_FF_TILE
    nf = D_FF // tf

    def ftile(s, f, nu):
        return jnp.where(s < nu[0], f, nf - 1)

    return pl.pallas_call(
        functools.partial(_moe_kernel, n_tok=n_tok, n_f=nf),
        grid_spec=pltpu.PrefetchScalarGridSpec(
            num_scalar_prefetch=5,
            grid=(n_sb, nf),
            in_specs=[pl.BlockSpec(memory_space=pl.ANY),
                      pl.BlockSpec((None, None, d, tf), lambda s, f, it, e, st, c, nu: (l, e[s], 0, ftile(s, f, nu))),
                      pl.BlockSpec((None, None, d, tf), lambda s, f, it, e, st, c, nu: (l, e[s], 0, nf + ftile(s, f, nu))),
                      pl.BlockSpec((None, None, tf, d), lambda s, f, it, e, st, c, nu: (l, e[s], ftile(s, f, nu), 0)),
                      pl.BlockSpec((None, None, 1, tf), lambda s, f, it, e, st, c, nu: (l, e[s], 0, ftile(s, f, nu))),
                      pl.BlockSpec((None, None, 1, tf), lambda s, f, it, e, st, c, nu: (l, e[s], 0, nf + ftile(s, f, nu))),
                      pl.BlockSpec((None, None, 1, d), lambda s, f, it, e, st, c, nu: (l, e[s], 0, 0))],
            out_specs=pl.BlockSpec(memory_space=pl.ANY),
            scratch_shapes=[pltpu.VMEM((MOE_ROWS, d), F32),
                            pltpu.VMEM((MOE_ROWS, d), BF16),
                            pltpu.VMEM((d, tf), BF16),
                            pltpu.VMEM((d, tf), BF16),
                            pltpu.VMEM((tf, d), BF16),
                            pltpu.SemaphoreType.DMA(())]),
        out_shape=jax.ShapeDtypeStruct((TOP_K * n_tok, d), F32),
        compiler_params=_cparams(("arbitrary", "arbitrary"), 56),
        name="moe_experts",
    )(items, sb_e, sb_start, sb_cnt, n_used, h, w_gu, w_gu, w_dn, b_gu, b_gu, b_dn)


def _route(top_idx):
    n_items = top_idx.shape[0] * TOP_K
    flat_e = top_idx.reshape(-1)
    items = jnp.argsort(flat_e).astype(jnp.int32)
    experts = jnp.arange(N_EXPERTS, dtype=jnp.int32)
    counts = jnp.sum((flat_e[:, None] == experts[None, :]).astype(jnp.int32), axis=0)
    starts = jnp.cumsum(counts) - counts
    groups = (counts + MOE_ROWS - 1) // MOE_ROWS
    group_end = jnp.cumsum(groups)
    n_used = group_end[-1]
    n_sb = n_items // MOE_ROWS + N_EXPERTS
    s = jnp.minimum(jnp.arange(n_sb, dtype=jnp.int32), n_used - 1)
    sb_e = jnp.sum((group_end[None, :] <= s[:, None]).astype(jnp.int32), axis=1)
    j = s - (group_end[sb_e] - groups[sb_e])
    sb_start = starts[sb_e] + j * MOE_ROWS
    live = jnp.arange(n_sb, dtype=jnp.int32) < n_used
    sb_cnt = jnp.where(live, jnp.clip(counts[sb_e] - j * MOE_ROWS, 0, MOE_ROWS), 0)
    i32 = lambda a: a.astype(jnp.int32)
    return items, i32(sb_e), i32(sb_start), i32(sb_cnt), i32(n_used).reshape(1)


def _combine_kernel(x_ref, y0_ref, y1_ref, y2_ref, y3_ref, g_ref, gf_ref, o_ref):
    g = g_ref[...]
    acc = y0_ref[...] * g[:, 0:1]
    for kk, y_ref in enumerate((y1_ref, y2_ref, y3_ref), start=1):
        acc = acc + y_ref[...] * g[:, kk:kk + 1]
    o_ref[...] = x_ref[...] + gf_ref[...] * acc


def _combine(x_all, y_items, gates, mod_tab, l, n_ctx, dec_seq):
    n, d = x_all.shape
    tm = ROW_TILE
    nt = n // tm
    slab = lambda kk: pl.BlockSpec((tm, d), lambda i: (kk * nt + i, 0))
    return pl.pallas_call(
        _combine_kernel,
        grid=(nt,),
        in_specs=[pl.BlockSpec((tm, d), lambda i: (i, 0)),
                  slab(0), slab(1), slab(2), slab(3),
                  pl.BlockSpec((tm, LANES), lambda i: (i, 0)),
                  pl.BlockSpec((None, 1, d), _mod_index(l, 5, tm, n_ctx, dec_seq))],
        out_specs=pl.BlockSpec((tm, d), lambda i: (i, 0)),
        out_shape=jax.ShapeDtypeStruct((n, d), F32),
        compiler_params=_cparams(("parallel",), 48),
        name="moe_combine",
    )(x_all, y_items, y_items, y_items, y_items, gates, mod_tab)


def kernel(x_prompt, x_sample, cache_diff_k, cache_diff_v, state_gla, c, c_ctx, norm_attn_w, norm_ffn_w, w_ada, b_ada, w_in, q_norm_w, k_norm_w, diff_lambda_w, diff_out_norm_w, w_gla_a2, b_gla_a, gla_out_norm_w, w_out, router_w, router_b, w_gate_up, b_gate_up, w_down, b_down):
    batch, seq, d = x_prompt.shape
    dec_batch, dec_seq, _ = x_sample.shape
    depth = w_in.shape[0]
    past = cache_diff_k.shape[2]
    n_ctx = batch * seq
    n_lat = dec_batch * dec_seq
    assert d == D_MODEL and 1 + dec_batch <= 8
    assert seq % ROW_TILE == 0 and dec_seq % ROW_TILE == 0 and past % ROW_TILE == 0
    tm_proj = 512 if (n_ctx % 512 == 0 and dec_seq % 512 == 0) else ROW_TILE

    cond8 = jnp.zeros((8, d), F32).at[0].set(c_ctx).at[1:1 + dec_batch].set(c)
    w_main = w_in[:, :, :PROJ_W].astype(BF16)
    w_ga = w_in[:, :, PROJ_W:].reshape(depth, d, 2, GLA_GATE_RANK).transpose(0, 2, 1, 3)
    w_ga = jnp.pad(w_ga, ((0, 0), (0, 0), (0, 0), (0, LANES - GLA_GATE_RANK))).astype(BF16)
    a2_pad = jnp.pad(w_gla_a2, ((0, 0), (0, 0), (0, LANES - GLA_GATE_RANK), (0, 0)))
    b_a = b_gla_a.reshape(depth, 2, 1, GLA_KW)
    qw = jnp.tile(q_norm_w, (1, 2)).reshape(depth, 1, HEAD_W)
    kw = jnp.tile(k_norm_w, (1, 2)).reshape(depth, 1, HEAD_W)
    w_out_b = w_out.astype(BF16)
    rw_pad = jnp.pad(router_w, ((0, 0), (0, 0), (0, LANES - N_EXPERTS))).astype(BF16)
    rb_pad = jnp.pad(router_b, ((0, 0), (0, LANES - N_EXPERTS)), constant_values=NEG_BIG).reshape(depth, 1, LANES)
    b_gu = b_gate_up.reshape(depth, N_EXPERTS, 1, 2 * D_FF)
    b_dn = b_down.reshape(depth, N_EXPERTS, 1, d)
    cos_t, sin_t = _rope_tables(dec_seq)
    cache_k = cache_diff_k.reshape(dec_batch, depth, past, DIFF_W)
    cache_v = cache_diff_v.reshape(dec_batch, depth, past, DIFF_W)

    mod_tab = _adaln(cond8, w_ada, b_ada).reshape(depth * 8 * 6, 1, d)
    x_all = jnp.concatenate([x_prompt.reshape(n_ctx, d), x_sample.reshape(n_lat, d)], axis=0)
    zero_state = jnp.zeros((batch, N_GLA_HEADS, GLA_DK, GLA_DV), F32)

    ks, vs, sts = [], [], []
    for l in range(depth):
        lam_init = 0.8 - 0.6 * math.exp(-0.3 * l)
        proj, la = _in_proj(x_all, norm_attn_w.reshape(depth, 1, d), mod_tab, w_main, w_ga, a2_pad, b_a,
                            l, n_ctx, dec_seq, tm_proj)
        qh, kt, vh, kc = _prep_ctx(proj, qw[l], kw[l], batch, seq)
        od_ctx = _attention(qh, kt, vh, diff_lambda_w[l], diff_out_norm_w[l].reshape(1, HEAD_W),
                            lam_init, ROW_TILE)
        og_ctx, s_f, s_b = _gla(proj, la, zero_state, zero_state, gla_out_norm_w[l].reshape(1, GLA_DV),
                                0, batch, seq, True)
        ks.append(kc.reshape(batch, seq, N_DIFF_HEADS, 2, DIFF_HEAD_DIM))
        vs.append(proj[:n_ctx, 2 * DIFF_W:3 * DIFF_W].reshape(batch, seq, N_DIFF_HEADS, HEAD_W))
        sts.append(jnp.stack([s_f, s_b], axis=1))
        qh, kt, vh = _prep_lat(proj, cache_k, cache_v, qw[l], kw[l], cos_t, sin_t, l, n_ctx,
                               dec_batch, dec_seq, past)
        od_lat = _attention(qh, kt, vh, diff_lambda_w[l], diff_out_norm_w[l].reshape(1, HEAD_W),
                            lam_init, ROW_TILE)
        og_lat, _, _ = _gla(proj, la, state_gla[:, l, 0], state_gla[:, l, 1],
                            gla_out_norm_w[l].reshape(1, GLA_DV), n_ctx, dec_batch, dec_seq, False)
        o_diff = jnp.concatenate([od_ctx.reshape(n_ctx, DIFF_W), od_lat.reshape(n_lat, DIFF_W)], axis=0)
        o_gla = jnp.concatenate([og_ctx, og_lat], axis=0)
        x_all, h2, top_idx, gates = _out_proj(x_all, o_diff, o_gla, w_out_b, mod_tab,
                                              norm_ffn_w.reshape(depth, 1, d), rw_pad, rb_pad,
                                              l, n_ctx, dec_seq)
        items, sb_e, sb_start, sb_cnt, n_used = _route(top_idx[:, :TOP_K])
        y_items = _experts(h2, items, sb_e, sb_start, sb_cnt, n_used, w_gate_up, b_gu, w_down, b_dn, l)
        x_all = _combine(x_all, y_items, gates, mod_tab, l, n_ctx, dec_seq)

    y_prompt = x_all[:n_ctx].reshape(batch, seq, d)
    y_sample = x_all[n_ctx:].reshape(dec_batch, dec_seq, d)
    return (y_prompt, y_sample, jnp.stack(ks, axis=1), jnp.stack(vs, axis=1), jnp.stack(sts, axis=1))
```

```python
import functools
import math

import jax
import jax.numpy as jnp
from jax import lax
from jax.experimental import pallas as pl
from jax.experimental.pallas import tpu as pltpu

F32 = jnp.float32
BF16 = jnp.bfloat16

D_MODEL = 2048
GRID_W = 64
N_DIFF_HEADS = 8
DIFF_HEAD_DIM = 64
HEAD_W = 2 * DIFF_HEAD_DIM
ROPE_AXIS_DIM = DIFF_HEAD_DIM // 2
ROPE_BASE = 10000.0
N_GLA_HEADS = 4
GLA_DK = 128
GLA_DV = 256
GLA_GATE_RANK = 16
GLA_GATE_TAU = 16.0
GLA_CHUNK = 64
N_EXPERTS = 32
TOP_K = 4
D_FF = D_MODEL
SWIGLU_LIMIT = 7.0
SWIGLU_ALPHA = 1.702
EPS = 1e-6

DIFF_W = N_DIFF_HEADS * HEAD_W
GLA_KW = N_GLA_HEADS * GLA_DK
GLA_VW = N_GLA_HEADS * GLA_DV
PROJ_W = 3 * DIFF_W + 2 * GLA_KW + 2 * GLA_VW
LANES = 128
ROW_TILE = 256
GLA_ROWS = 1024
MOE_ROWS = 1024
MOE_SUB = 512
MOE_FF_TILE = 512
DMA_UNROLL_LOG2 = 3
DMA_UNROLL = 1 << DMA_UNROLL_LOG2
TOP_K_LOG2 = 2
NEG_BIG = -1e30
Q_SCALE = DIFF_HEAD_DIM ** -0.5 * math.log2(math.e)


def _cparams(sem, vmem_mb):
    return pltpu.CompilerParams(dimension_semantics=sem, vmem_limit_bytes=vmem_mb * 2**20)


def _dot(a, b):
    return jnp.dot(a, b, preferred_element_type=F32)


def _split_dot(a, b, dims=None):
    hi = a.astype(BF16)
    lo = (a - hi.astype(F32)).astype(BF16)
    if dims is None:
        return _dot(hi, b) + _dot(lo, b)
    return (lax.dot_general(hi, b, dims, preferred_element_type=F32)
            + lax.dot_general(lo, b, dims, preferred_element_type=F32))


def _adaln_kernel(c_ref, w_ref, b_ref, o_ref):
    c = c_ref[...]
    s = c * (1.0 / (1.0 + jnp.exp(-c)))
    o_ref[...] = _dot(s.astype(BF16), w_ref[...].astype(BF16)) + b_ref[...]


def _adaln(cond8, w_ada, b_ada):
    depth, d, n = w_ada.shape
    tn = 1024
    return pl.pallas_call(
        _adaln_kernel,
        grid=(depth, n // tn),
        in_specs=[pl.BlockSpec((8, d), lambda l, j: (0, 0)),
                  pl.BlockSpec((None, d, tn), lambda l, j: (l, 0, j)),
                  pl.BlockSpec((None, 1, tn), lambda l, j: (l, 0, j))],
        out_specs=pl.BlockSpec((None, 8, tn), lambda l, j: (l, 0, j)),
        out_shape=jax.ShapeDtypeStruct((depth, 8, n), F32),
        compiler_params=_cparams(("parallel", "parallel"), 40),
        name="adaln",
    )(cond8, w_ada, b_ada.reshape(depth, 1, n))


def _mod_index(l, chunk, tm, n_ctx, dec_seq):
    def index_map(i, *_):
        r = jnp.where(i * tm < n_ctx, 0, 1 + (i * tm - n_ctx) // dec_seq)
        return ((l * 8 + r) * 6 + chunk, 0, 0)
    return index_map


def _in_proj_kernel(x_ref, nw_ref, sh_ref, sc_ref, w_ref, wga_ref, a2_ref, ba_ref,
                    o_ref, la_ref, h_scr):
    @pl.when(pl.program_id(1) == 0)
    def _():
        x = x_ref[...]
        y = x * lax.rsqrt(jnp.mean(x * x, axis=-1, keepdims=True) + EPS) * nw_ref[...]
        hb = (y * (1.0 + sc_ref[...]) + sh_ref[...]).astype(BF16)
        h_scr[...] = hb
        for m in range(2):
            ga = _dot(hb, wga_ref[m])
            z = _dot(ga.astype(BF16), a2_ref[m].astype(BF16)) + ba_ref[m]
            ls = jnp.minimum(z, 0.0) - jnp.log1p(jnp.exp(-jnp.abs(z)))
            la_ref[:, m * GLA_KW:(m + 1) * GLA_KW] = ls * (1.0 / GLA_GATE_TAU)

    o_ref[...] = _dot(h_scr[...], w_ref[...])


def _in_proj(x_all, norm_w, mod_tab, w_main, w_ga, a2_pad, b_a, l, n_ctx, dec_seq, tm):
    n, d = x_all.shape
    tn = 1024
    return pl.pallas_call(
        _in_proj_kernel,
        grid=(n // tm, PROJ_W // tn),
        in_specs=[pl.BlockSpec((tm, d), lambda i, j: (i, 0)),
                  pl.BlockSpec((None, 1, d), lambda i, j: (l, 0, 0)),
                  pl.BlockSpec((None, 1, d), _mod_index(l, 0, tm, n_ctx, dec_seq)),
                  pl.BlockSpec((None, 1, d), _mod_index(l, 1, tm, n_ctx, dec_seq)),
                  pl.BlockSpec((None, d, tn), lambda i, j: (l, 0, j)),
                  pl.BlockSpec((None, 2, d, LANES), lambda i, j: (l, 0, 0, 0)),
                  pl.BlockSpec((None, 2, LANES, GLA_KW), lambda i, j: (l, 0, 0, 0)),
                  pl.BlockSpec((None, 2, 1, GLA_KW), lambda i, j: (l, 0, 0, 0))],
        out_specs=[pl.BlockSpec((tm, tn), lambda i, j: (i, j)),
                   pl.BlockSpec((tm, 2 * GLA_KW), lambda i, j: (i, 0))],
        out_shape=[jax.ShapeDtypeStruct((n, PROJ_W), F32),
                   jax.ShapeDtypeStruct((n, 2 * GLA_KW), F32)],
        scratch_shapes=[pltpu.VMEM((tm, d), BF16)],
        compiler_params=_cparams(("parallel", "arbitrary"), 48),
        name="in_proj",
    )(x_all, norm_w, mod_tab, mod_tab, w_main, w_ga, a2_pad, b_a)


def _group_ones():
    r = lax.broadcasted_iota(jnp.int32, (HEAD_W, HEAD_W), 0) // DIFF_HEAD_DIM
    c = lax.broadcasted_iota(jnp.int32, (HEAD_W, HEAD_W), 1) // DIFF_HEAD_DIM
    return (r == c).astype(BF16)


def _head_norm(x, w, g):
    ss = _split_dot(x * x, g)
    return x * lax.rsqrt(ss * (1.0 / DIFF_HEAD_DIM) + EPS) * w


def _rope(y, cos, sin_signed, lo_mask):
    half = ROPE_AXIS_DIM // 2
    partner = jnp.where(lo_mask, pltpu.roll(y, HEAD_W - half, 1), pltpu.roll(y, half, 1))
    return y * cos + partner * sin_signed


def _prep_ctx_kernel(q_ref, k_ref, v_ref, qw_ref, kw_ref, qh_ref, kt_ref, vh_ref, kc_ref):
    g = _group_ones()
    for h in range(N_DIFF_HEADS):
        sl = slice(h * HEAD_W, (h + 1) * HEAD_W)
        qn = _head_norm(q_ref[:, sl], qw_ref[...], g)
        kn = _head_norm(k_ref[:, sl], kw_ref[...], g)
        qh_ref[:, sl] = (qn * Q_SCALE).astype(BF16)
        kc_ref[:, sl] = kn
        kt_ref[h] = kn.T.astype(BF16)
    vh_ref[...] = v_ref[...].astype(BF16)


def _prep_ctx(proj, qw, kw, batch, seq):
    tm = ROW_TILE
    nt = seq // tm
    row = lambda c: (lambda b, j: (b * nt + j, c))
    return pl.pallas_call(
        _prep_ctx_kernel,
        grid=(batch, nt),
        in_specs=[pl.BlockSpec((tm, DIFF_W), row(0)),
                  pl.BlockSpec((tm, DIFF_W), row(1)),
                  pl.BlockSpec((tm, DIFF_W), row(2)),
                  pl.BlockSpec((1, HEAD_W), lambda b, j: (0, 0)),
                  pl.BlockSpec((1, HEAD_W), lambda b, j: (0, 0))],
        out_specs=[pl.BlockSpec((None, tm, DIFF_W), lambda b, j: (b, j, 0)),
                   pl.BlockSpec((None, N_DIFF_HEADS, HEAD_W, tm), lambda b, j: (b, 0, 0, j)),
                   pl.BlockSpec((None, tm, DIFF_W), lambda b, j: (b, j, 0)),
                   pl.BlockSpec((None, tm, DIFF_W), lambda b, j: (b, j, 0))],
        out_shape=[jax.ShapeDtypeStruct((batch, seq, DIFF_W), BF16),
                   jax.ShapeDtypeStruct((batch, N_DIFF_HEADS, HEAD_W, seq), BF16),
                   jax.ShapeDtypeStruct((batch, seq, DIFF_W), BF16),
                   jax.ShapeDtypeStruct((batch, seq, DIFF_W), F32)],
        compiler_params=_cparams(("parallel", "parallel"), 40),
        name="prep_ctx",
    )(proj, proj, proj, qw, kw)


def _prep_lat_kernel(q_ref, k_ref, v_ref, ck_ref, cv_ref, qw_ref, kw_ref, cos_ref, sin_ref,
                     qh_ref, kt_ref, vh_ref, *, n_lat_tiles):
    j = pl.program_id(1)

    @pl.when(j < n_lat_tiles)
    def _():
        g = _group_ones()
        cos = cos_ref[...]
        sin = sin_ref[...]
        lane = lax.broadcasted_iota(jnp.int32, cos.shape, 1)
        lo_mask = (lane % ROPE_AXIS_DIM) < (ROPE_AXIS_DIM // 2)
        for h in range(N_DIFF_HEADS):
            sl = slice(h * HEAD_W, (h + 1) * HEAD_W)
            qn = _rope(_head_norm(q_ref[:, sl], qw_ref[...], g), cos, sin, lo_mask)
            kn = _rope(_head_norm(k_ref[:, sl], kw_ref[...], g), cos, sin, lo_mask)
            qh_ref[:, sl] = (qn * Q_SCALE).astype(BF16)
            kt_ref[h] = kn.T.astype(BF16)
        vh_ref[...] = v_ref[...].astype(BF16)

    @pl.when(j >= n_lat_tiles)
    def _():
        for h in range(N_DIFF_HEADS):
            kt_ref[h] = ck_ref[:, h * HEAD_W:(h + 1) * HEAD_W].T.astype(BF16)
        vh_ref[...] = cv_ref[...].astype(BF16)


def _prep_lat(proj, cache_k, cache_v, qw, kw, cos_t, sin_t, l, n_ctx, dec_batch, dec_seq, past):
    tm = ROW_TILE
    nl = dec_seq // tm
    npast = past // tm
    row0 = n_ctx // tm
    row = lambda c: (lambda b, j: (row0 + b * nl + jnp.minimum(j, nl - 1), c))
    cache = lambda b, j: (b, l, jnp.maximum(j - nl, 0), 0)
    tab = lambda b, j: (jnp.minimum(j, nl - 1), 0)
    return pl.pallas_call(
        functools.partial(_prep_lat_kernel, n_lat_tiles=nl),
        grid=(dec_batch, nl + npast),
        in_specs=[pl.BlockSpec((tm, DIFF_W), row(0)),
                  pl.BlockSpec((tm, DIFF_W), row(1)),
                  pl.BlockSpec((tm, DIFF_W), row(2)),
                  pl.BlockSpec((None, None, tm, DIFF_W), cache),
                  pl.BlockSpec((None, None, tm, DIFF_W), cache),
                  pl.BlockSpec((1, HEAD_W), lambda b, j: (0, 0)),
                  pl.BlockSpec((1, HEAD_W), lambda b, j: (0, 0)),
                  pl.BlockSpec((tm, HEAD_W), tab),
                  pl.BlockSpec((tm, HEAD_W), tab)],
        out_specs=[pl.BlockSpec((None, tm, DIFF_W), lambda b, j: (b, jnp.minimum(j, nl - 1), 0)),
                   pl.BlockSpec((None, N_DIFF_HEADS, HEAD_W, tm), lambda b, j: (b, 0, 0, j)),
                   pl.BlockSpec((None, tm, DIFF_W), lambda b, j: (b, j, 0))],
        out_shape=[jax.ShapeDtypeStruct((dec_batch, dec_seq, DIFF_W), BF16),
                   jax.ShapeDtypeStruct((dec_batch, N_DIFF_HEADS, HEAD_W, dec_seq + past), BF16),
                   jax.ShapeDtypeStruct((dec_batch, dec_seq + past, DIFF_W), BF16)],
        compiler_params=_cparams(("parallel", "arbitrary"), 40),
        name="prep_lat",
    )(proj, proj, proj, cache_k, cache_v, qw, kw, cos_t, sin_t)


def _rope_tables(n_tok):
    t = jnp.arange(n_tok)
    r = (t // GRID_W).astype(F32)
    c = (t % GRID_W).astype(F32)
    inv = ROPE_BASE ** (-jnp.arange(0, ROPE_AXIS_DIM, 2, dtype=F32) / ROPE_AXIS_DIM)
    ar = r[:, None] * inv
    ac = c[:, None] * inv
    cos64 = jnp.concatenate([jnp.cos(ar), jnp.cos(ar), jnp.cos(ac), jnp.cos(ac)], axis=-1)
    sin64 = jnp.concatenate([-jnp.sin(ar), jnp.sin(ar), -jnp.sin(ac), jnp.sin(ac)], axis=-1)
    return jnp.tile(cos64, (1, 2)), jnp.tile(sin64, (1, 2))


def _attn_kernel(q_ref, kt_ref, v_ref, lw_ref, ow_ref, o_ref, *, lam_init):
    q = q_ref[...]
    lane = lax.broadcasted_iota(jnp.int32, q.shape, 1)
    kt = kt_ref[...]
    v = v_ref[...]

    def attend(qm):
        s = _dot(qm, kt)
        p = jnp.exp2(s - jnp.max(s, axis=-1, keepdims=True))
        denom = jnp.sum(p, axis=-1, keepdims=True)
        return _dot(p.astype(BF16), v) / denom

    o1 = attend(jnp.where(lane < DIFF_HEAD_DIM, q, jnp.zeros_like(q)))
    o2 = attend(jnp.where(lane >= DIFF_HEAD_DIM, q, jnp.zeros_like(q)))
    lw = lw_ref[...]
    lam = (jnp.exp(jnp.sum(lw[0:1] * lw[1:2], axis=-1, keepdims=True))
           - jnp.exp(jnp.sum(lw[2:3] * lw[3:4], axis=-1, keepdims=True)) + lam_init)
    o = o1 - lam * o2
    y = o * lax.rsqrt(jnp.mean(o * o, axis=-1, keepdims=True) + EPS) * ow_ref[...]
    o_ref[...] = (y * (1.0 - lam_init)).astype(BF16)


def _attention(qh, kt, vh, lam_w, out_w, lam_init, tq):
    batch, tq_total, _ = qh.shape
    tk = kt.shape[-1]
    return pl.pallas_call(
        functools.partial(_attn_kernel, lam_init=lam_init),
        grid=(batch, N_DIFF_HEADS, tq_total // tq),
        in_specs=[pl.BlockSpec((None, tq, HEAD_W), lambda b, h, i: (b, i, h)),
                  pl.BlockSpec((None, None, HEAD_W, tk), lambda b, h, i: (b, h, 0, 0)),
                  pl.BlockSpec((None, tk, HEAD_W), lambda b, h, i: (b, 0, h)),
                  pl.BlockSpec((4, DIFF_HEAD_DIM), lambda b, h, i: (0, 0)),
                  pl.BlockSpec((1, HEAD_W), lambda b, h, i: (0, 0))],
        out_specs=pl.BlockSpec((None, tq, HEAD_W), lambda b, h, i: (b, i, h)),
        out_shape=jax.ShapeDtypeStruct((batch, tq_total, DIFF_W), BF16),
        compiler_params=_cparams(("parallel", "parallel", "parallel"), 48),
        name="diff_attn",
    )(qh, kt, vh, lam_w, out_w)


def _gla_kernel(*refs, reverse, final, zero_init, n_chunks):
    if final:
        q_ref, k_ref, v_ref, la_ref, s0_ref, of_ref, gg_ref, gw_ref, o_ref, s_ref, state = refs
    else:
        q_ref, k_ref, v_ref, la_ref, s0_ref, o_ref, s_ref, state = refs
    t = pl.program_id(2)

    @pl.when(t == 0)
    def _():
        state[...] = jnp.zeros_like(state) if zero_init else s0_ref[...]

    r = lax.broadcasted_iota(jnp.int32, (GLA_CHUNK, GLA_CHUNK), 0)
    c = lax.broadcasted_iota(jnp.int32, (GLA_CHUNK, GLA_CHUNK), 1)
    tri = (c >= r) if reverse else (c <= r)
    tri_b = tri.astype(BF16)
    ones_b = jnp.ones((GLA_CHUNK, GLA_DK), BF16)
    contract_rows = (((0,), (0,)), ((), ()))
    chunks = range(n_chunks - 1, -1, -1) if reverse else range(n_chunks)
    for ci in chunks:
        rows = slice(ci * GLA_CHUNK, (ci + 1) * GLA_CHUNK)
        la = la_ref[rows, :]
        hi = la.astype(BF16)
        lo = (la - hi.astype(F32)).astype(BF16)
        b = _dot(tri_b, hi) + _dot(tri_b, lo)
        b_last = b[0:1] if reverse else b[GLA_CHUNK - 1:GLA_CHUNK]
        q = q_ref[rows, :] * GLA_DK ** -0.5
        k = k_ref[rows, :]
        v = v_ref[rows, :].astype(BF16)
        q_dec = (q * jnp.exp(b)).astype(BF16)
        k_inv = (k * jnp.exp(-b)).astype(BF16)
        k_end = (k * jnp.exp(b_last - b)).astype(BF16)
        att = lax.dot_general(q_dec, k_inv, (((1,), (1,)), ((), ())), preferred_element_type=F32)
        att = jnp.where(tri, att, 0.0).astype(BF16)
        s_prev = state[...]
        o = _dot(att, v) + _dot(q_dec, s_prev.astype(BF16))
        u = lax.dot_general(k_end, v, contract_rows, preferred_element_type=F32)
        tot = (lax.dot_general(hi, ones_b, contract_rows, preferred_element_type=F32)
               + lax.dot_general(lo, ones_b, contract_rows, preferred_element_type=F32))
        decay = jnp.exp(tot)
        state[...] = jnp.concatenate([decay, decay], axis=1) * s_prev + u
        if final:
            tot_o = o + of_ref[rows, :]
            y = tot_o * lax.rsqrt(jnp.mean(tot_o * tot_o, axis=-1, keepdims=True) + EPS) * gw_ref[...]
            gg = gg_ref[rows, :]
            y = y * (gg * (1.0 / (1.0 + jnp.exp(-gg))))
            o_ref[rows, :] = y.astype(BF16)
        else:
            o_ref[rows, :] = o

    @pl.when(t == pl.num_programs(2) - 1)
    def _():
        s_ref[...] = state[...]


def _gla_pass(proj, la, s0, o_fwd, gla_w, row0, batch, seq, direction, zero_init):
    tb = GLA_ROWS if (seq % GLA_ROWS == 0 and row0 % GLA_ROWS == 0) else ROW_TILE
    assert seq % tb == 0 and row0 % tb == 0
    nt = seq // tb
    reverse = direction == 1
    final = o_fwd is not None
    tt = (lambda t: nt - 1 - t) if reverse else (lambda t: t)
    rb0 = row0 // tb
    q_c0 = 3 * DIFF_W // GLA_DK
    k_c0 = (3 * DIFF_W + GLA_KW) // GLA_DK
    v_c0 = (3 * DIFF_W + 2 * GLA_KW) // GLA_DV
    g_c0 = (3 * DIFF_W + 2 * GLA_KW + GLA_VW) // GLA_DV
    la_c0 = direction * N_GLA_HEADS
    prow = lambda c0: (lambda b, h, t: (rb0 + b * nt + tt(t), c0 + h))
    lrow = lambda c0: (lambda b, h, t: (b * nt + tt(t), c0 + h))
    in_specs = [pl.BlockSpec((tb, GLA_DK), prow(q_c0)),
                pl.BlockSpec((tb, GLA_DK), prow(k_c0)),
                pl.BlockSpec((tb, GLA_DV), prow(v_c0)),
                pl.BlockSpec((tb, GLA_DK), prow(la_c0)),
                pl.BlockSpec((None, None, GLA_DK, GLA_DV), lambda b, h, t: (b, h, 0, 0))]
    args = [proj, proj, proj, la, s0]
    if final:
        in_specs += [pl.BlockSpec((tb, GLA_DV), lrow(0)),
                     pl.BlockSpec((tb, GLA_DV), prow(g_c0)),
                     pl.BlockSpec((1, GLA_DV), lambda b, h, t: (0, 0))]
        args += [o_fwd, proj, gla_w]
    n_rows = batch * seq
    return pl.pallas_call(
        functools.partial(_gla_kernel, reverse=reverse, final=final, zero_init=zero_init,
                          n_chunks=tb // GLA_CHUNK),
        grid=(batch, N_GLA_HEADS, nt),
        in_specs=in_specs,
        out_specs=[pl.BlockSpec((tb, GLA_DV), lrow(0)),
                   pl.BlockSpec((None, None, GLA_DK, GLA_DV), lambda b, h, t: (b, h, 0, 0))],
        out_shape=[jax.ShapeDtypeStruct((n_rows, GLA_VW), BF16 if final else F32),
                   jax.ShapeDtypeStruct((batch, N_GLA_HEADS, GLA_DK, GLA_DV), F32)],
        scratch_shapes=[pltpu.VMEM((GLA_DK, GLA_DV), F32)],
        compiler_params=_cparams(("parallel", "parallel", "arbitrary"), 40),
        name="gla_bwd" if reverse else "gla_fwd",
    )(*args)


def _gla(proj, la, s0_f, s0_b, gla_w, row0, batch, seq, zero_init):
    o_f, s_f = _gla_pass(proj, la, s0_f, None, None, row0, batch, seq, 0, zero_init)
    o, s_b = _gla_pass(proj, la, s0_b, o_f, gla_w, row0, batch, seq, 1, zero_init)
    return o, s_f, s_b


def _out_proj_kernel(x_ref, odc_ref, odl_ref, ogc_ref, ogl_ref, w_ref, ga_ref, nw_ref, sh_ref, sc_ref,
                     rw_ref, rb_ref, xo_ref, h_ref, idx_ref, gate_ref, *, n_ctx_tiles):
    is_ctx = pl.program_id(0) < n_ctx_tiles
    od = jnp.where(is_ctx, odc_ref[...], odl_ref[...])
    og = jnp.where(is_ctx, ogc_ref[...], ogl_ref[...])
    mixed = _dot(od, w_ref[0:DIFF_W, :]) + _dot(og, w_ref[DIFF_W:, :])
    x = x_ref[...] + ga_ref[...] * mixed
    xo_ref[...] = x
    y = x * lax.rsqrt(jnp.mean(x * x, axis=-1, keepdims=True) + EPS) * nw_ref[...]
    h = y * (1.0 + sc_ref[...]) + sh_ref[...]
    h_ref[...] = h
    logits = _dot(h.astype(BF16), rw_ref[...]) + rb_ref[...]
    lane = lax.broadcasted_iota(jnp.int32, logits.shape, 1)
    lane_f = lane.astype(F32)
    idx_out = jnp.zeros(logits.shape, F32)
    vals = []
    for kk in range(TOP_K):
        m = jnp.max(logits, axis=-1, keepdims=True)
        idx = jnp.min(jnp.where(logits == m, lane_f, float(LANES)), axis=-1, keepdims=True)
        logits = jnp.where(lane_f == idx, 2.0 * NEG_BIG, logits)
        idx_out = jnp.where(lane == kk, idx, idx_out)
        vals.append(m)
    es = [jnp.exp(vv - vals[0]) for vv in vals]
    denom = es[0] + es[1] + es[2] + es[3]
    gate_out = jnp.zeros(logits.shape, F32)
    for kk in range(TOP_K):
        gate_out = jnp.where(lane == kk, es[kk] / denom, gate_out)
    idx_ref[...] = idx_out.astype(jnp.int32)
    gate_ref[...] = gate_out


def _out_proj(x_all, od_ctx, od_lat, og_ctx, og_lat, w_out_b, mod_tab, norm_w, rw_pad, rb_pad, l, n_ctx, dec_seq):
    n, d = x_all.shape
    tm = ROW_TILE
    nc = n_ctx // tm
    nl = (n - n_ctx) // tm
    mi = lambda c: _mod_index(l, c, tm, n_ctx, dec_seq)
    ctx_rows = lambda i: (jnp.minimum(i, nc - 1), 0)
    lat_rows = lambda i: (jnp.clip(i - nc, 0, nl - 1), 0)
    return pl.pallas_call(
        functools.partial(_out_proj_kernel, n_ctx_tiles=nc),
        grid=(n // tm,),
        in_specs=[pl.BlockSpec((tm, d), lambda i: (i, 0)),
                  pl.BlockSpec((tm, DIFF_W), ctx_rows),
                  pl.BlockSpec((tm, DIFF_W), lat_rows),
                  pl.BlockSpec((tm, GLA_VW), ctx_rows),
                  pl.BlockSpec((tm, GLA_VW), lat_rows),
                  pl.BlockSpec((None, d, d), lambda i: (l, 0, 0)),
                  pl.BlockSpec((None, 1, d), mi(2)),
                  pl.BlockSpec((None, 1, d), lambda i: (l, 0, 0)),
                  pl.BlockSpec((None, 1, d), mi(3)),
                  pl.BlockSpec((None, 1, d), mi(4)),
                  pl.BlockSpec((None, d, LANES), lambda i: (l, 0, 0)),
                  pl.BlockSpec((None, 1, LANES), lambda i: (l, 0, 0))],
        out_specs=[pl.BlockSpec((tm, d), lambda i: (i, 0)),
                   pl.BlockSpec((tm, d), lambda i: (i, 0)),
                   pl.BlockSpec((tm, LANES), lambda i: (i, 0)),
                   pl.BlockSpec((tm, LANES), lambda i: (i, 0))],
        out_shape=[jax.ShapeDtypeStruct((n, d), F32),
                   jax.ShapeDtypeStruct((n, d), F32),
                   jax.ShapeDtypeStruct((n, LANES), jnp.int32),
                   jax.ShapeDtypeStruct((n, LANES), F32)],
        compiler_params=_cparams(("parallel",), 48),
        name="out_proj_router",
    )(x_all, od_ctx, od_lat, og_ctx, og_lat, w_out_b, mod_tab, norm_w, mod_tab, mod_tab, rw_pad, rb_pad)


def _row_copy(src, src_row, dst, dst_row, sem):
    return pltpu.make_async_copy(src.at[pl.ds(src_row, 1)], dst.at[pl.ds(dst_row, 1)], sem)


def _for_rows(cnt, body):
    n_full = lax.shift_right_logical(cnt, DMA_UNROLL_LOG2)

    def chunk(i, carry):
        for u in range(DMA_UNROLL):
            body(i * DMA_UNROLL + u)
        return carry
    lax.fori_loop(0, n_full, chunk, 0)

    def tail(p, carry):
        body(p)
        return carry
    lax.fori_loop(n_full * DMA_UNROLL, cnt, tail, 0)


def _item_token(item):
    return lax.shift_right_logical(item, TOP_K_LOG2)


def _item_k(item):
    return lax.bitwise_and(item, TOP_K - 1)


def _moe_kernel(items_ref, sbe_ref, sbs_ref, sbc_ref, nu_ref,
                h_hbm, wg_ref, wl_ref, wd_ref, bg_ref, bl_ref, bd_ref,
                y_hbm,
                acc, xb, wg_s, wl_s, wd_s, sem, *, n_tok, n_f):
    s = pl.program_id(0)
    f = pl.program_id(1)
    cnt = sbc_ref[s]
    start = sbs_ref[s]
    subs = [r * MOE_SUB for r in range(MOE_ROWS // MOE_SUB)]

    @pl.when(jnp.logical_and(s == 0, f == 0))
    def _():
        acc[...] = jnp.zeros_like(acc)

    @pl.when(jnp.logical_and(cnt > 0, f == 0))
    def _():
        _for_rows(cnt, lambda p: _row_copy(h_hbm, _item_token(items_ref[start + p]), acc, p, sem).start())
        _for_rows(cnt, lambda p: _row_copy(h_hbm, 0, acc, p, sem).wait())
        for r0 in subs:
            @pl.when(r0 < cnt)
            def _():
                xb[r0:r0 + MOE_SUB, :] = acc[r0:r0 + MOE_SUB, :].astype(BF16)

    def sub_block(r0):
        rows = slice(r0, r0 + MOE_SUB)
        x = xb[rows, :]
        glu = jnp.minimum(_dot(x, wg_s[...]) + bg_ref[...], SWIGLU_LIMIT)
        lin = jnp.clip(_dot(x, wl_s[...]) + bl_ref[...], -SWIGLU_LIMIT, SWIGLU_LIMIT)
        act = glu * (1.0 / (1.0 + jnp.exp(-SWIGLU_ALPHA * glu))) * (lin + 1.0)
        contrib = _dot(act.astype(BF16), wd_s[...])
        acc[rows, :] = jnp.where(f == 0, bd_ref[...], acc[rows, :]) + contrib

    @pl.when(cnt > 0)
    def _():
        wg_s[...] = wg_ref[...].astype(BF16)
        wl_s[...] = wl_ref[...].astype(BF16)
        wd_s[...] = wd_ref[...].astype(BF16)
        sub_block(subs[0])
        for r0 in subs[1:]:
            pl.when(r0 < cnt)(functools.partial(sub_block, r0))

    @pl.when(jnp.logical_and(cnt > 0, f == n_f - 1))
    def _():
        def scatter(p):
            item = items_ref[start + p]
            _row_copy(acc, p, y_hbm, _item_k(item) * n_tok + _item_token(item), sem).start()
        _for_rows(cnt, scatter)
        _for_rows(cnt, lambda p: _row_copy(acc, p, y_hbm, 0, sem).wait())


def _experts(h, items, sb_e, sb_start, sb_cnt, n_used, w_gu, b_gu, w_dn, b_dn, l):
    n_tok, d = h.shape
    n_sb = sb_e.shape[0]
    tf = MOE_FF_TILE
    nf = D_FF // tf

    def ftile(s, f, nu):
        return jnp.where(s < nu[0], f, nf - 1)

    return pl.pallas_call(
        functools.partial(_moe_kernel, n_tok=n_tok, n_f=nf),
        grid_spec=pltpu.PrefetchScalarGridSpec(
            num_scalar_prefetch=5,
            grid=(n_sb, nf),
            in_specs=[pl.BlockSpec(memory_space=pl.ANY),
                      pl.BlockSpec((None, None, d, tf), lambda s, f, it, e, st, c, nu: (l, e[s], 0, ftile(s, f, nu))),
                      pl.BlockSpec((None, None, d, tf), lambda s, f, it, e, st, c, nu: (l, e[s], 0, nf + ftile(s, f, nu))),
                      pl.BlockSpec((None, None, tf, d), lambda s, f, it, e, st, c, nu: (l, e[s], ftile(s, f, nu), 0)),
                      pl.BlockSpec((None, None, 1, tf), lambda s, f, it, e, st, c, nu: (l, e[s], 0, ftile(s, f, nu))),
                      pl.BlockSpec((None, None, 1, tf), lambda s, f, it, e, st, c, nu: (l, e[s], 0, nf + ftile(s, f, nu))),
                      pl.BlockSpec((None, None, 1, d), lambda s, f, it, e, st, c, nu: (l, e[s], 0, 0))],
            out_specs=pl.BlockSpec(memory_space=pl.ANY),
            scratch_shapes=[pltpu.VMEM((MOE_ROWS, d), F32),
                            pltpu.VMEM((MOE_ROWS, d), BF16),
                            pltpu.VMEM((d, tf), BF16),
                            pltpu.VMEM((d, tf), BF16),
                            pltpu.VMEM((tf, d), BF16),
                            pltpu.SemaphoreType.DMA(())]),
        out_shape=jax.ShapeDtypeStruct((TOP_K * n_tok, d), F32),
        compiler_params=_cparams(("arbitrary", "arbitrary"), 56),
        name="moe_experts",
    )(items, sb_e, sb_start, sb_cnt, n_used, h, w_gu, w_gu, w_dn, b_gu, b_gu, b_dn)


def _route(top_idx):
    n_items = top_idx.shape[0] * TOP_K
    flat_e = top_idx.reshape(-1)
    items = jnp.argsort(flat_e).astype(jnp.int32)
    experts = jnp.arange(N_EXPERTS, dtype=jnp.int32)
    counts = jnp.sum((flat_e[:, None] == experts[None, :]).astype(jnp.int32), axis=0)
    starts = jnp.cumsum(counts) - counts
    groups = (counts + MOE_ROWS - 1) // MOE_ROWS
    group_end = jnp.cumsum(groups)
    n_used = group_end[-1]
    n_sb = n_items // MOE_ROWS + N_EXPERTS
    s = jnp.minimum(jnp.arange(n_sb, dtype=jnp.int32), n_used - 1)
    sb_e = jnp.sum((group_end[None, :] <= s[:, None]).astype(jnp.int32), axis=1)
    j = s - (group_end[sb_e] - groups[sb_e])
    sb_start = starts[sb_e] + j * MOE_ROWS
    live = jnp.arange(n_sb, dtype=jnp.int32) < n_used
    sb_cnt = jnp.where(live, jnp.clip(counts[sb_e] - j * MOE_ROWS, 0, MOE_ROWS), 0)
    i32 = lambda a: a.astype(jnp.int32)
    return items, i32(sb_e), i32(sb_start), i32(sb_cnt), i32(n_used).reshape(1)


def _combine_kernel(x_ref, y0_ref, y1_ref, y2_ref, y3_ref, g_ref, gf_ref, o_ref):
    g = g_ref[...]
    acc = y0_ref[...] * g[:, 0:1]
    for kk, y_ref in enumerate((y1_ref, y2_ref, y3_ref), start=1):
        acc = acc + y_ref[...] * g[:, kk:kk + 1]
    o_ref[...] = x_ref[...] + gf_ref[...] * acc


def _combine(x_all, y_items, gates, mod_tab, l, n_ctx, dec_seq):
    n, d = x_all.shape
    tm = ROW_TILE
    nt = n // tm
    slab = lambda kk: pl.BlockSpec((tm, d), lambda i: (kk * nt + i, 0))
    return pl.pallas_call(
        _combine_kernel,
        grid=(nt,),
        in_specs=[pl.BlockSpec((tm, d), lambda i: (i, 0)),
                  slab(0), slab(1), slab(2), slab(3),
                  pl.BlockSpec((tm, LANES), lambda i: (i, 0)),
                  pl.BlockSpec((None, 1, d), _mod_index(l, 5, tm, n_ctx, dec_seq))],
        out_specs=pl.BlockSpec((tm, d), lambda i: (i, 0)),
        out_shape=jax.ShapeDtypeStruct((n, d), F32),
        compiler_params=_cparams(("parallel",), 48),
        name="moe_combine",
    )(x_all, y_items, y_items, y_items, y_items, gates, mod_tab)


def kernel(x_prompt, x_sample, cache_diff_k, cache_diff_v, state_gla, c, c_ctx, norm_attn_w, norm_ffn_w, w_ada, b_ada, w_in, q_norm_w, k_norm_w, diff_lambda_w, diff_out_norm_w, w_gla_a2, b_gla_a, gla_out_norm_w, w_out, router_w, router_b, w_gate_up, b_gate_up, w_down, b_down):
    batch, seq, d = x_prompt.shape
    dec_batch, dec_seq, _ = x_sample.shape
    depth = w_in.shape[0]
    past = cache_diff_k.shape[2]
    n_ctx = batch * seq
    n_lat = dec_batch * dec_seq
    assert d == D_MODEL and 1 + dec_batch <= 8
    assert seq % ROW_TILE == 0 and dec_seq % ROW_TILE == 0 and past % ROW_TILE == 0
    tm_proj = 512 if (n_ctx % 512 == 0 and dec_seq % 512 == 0) else ROW_TILE

    cond8 = jnp.zeros((8, d), F32).at[0].set(c_ctx).at[1:1 + dec_batch].set(c)
    w_main = w_in[:, :, :PROJ_W].astype(BF16)
    w_ga = w_in[:, :, PROJ_W:].reshape(depth, d, 2, GLA_GATE_RANK).transpose(0, 2, 1, 3)
    w_ga = jnp.pad(w_ga, ((0, 0), (0, 0), (0, 0), (0, LANES - GLA_GATE_RANK))).astype(BF16)
    a2_pad = jnp.pad(w_gla_a2, ((0, 0), (0, 0), (0, LANES - GLA_GATE_RANK), (0, 0)))
    b_a = b_gla_a.reshape(depth, 2, 1, GLA_KW)
    qw = jnp.tile(q_norm_w, (1, 2)).reshape(depth, 1, HEAD_W)
    kw = jnp.tile(k_norm_w, (1, 2)).reshape(depth, 1, HEAD_W)
    w_out_b = w_out.astype(BF16)
    rw_pad = jnp.pad(router_w, ((0, 0), (0, 0), (0, LANES - N_EXPERTS))).astype(BF16)
    rb_pad = jnp.pad(router_b, ((0, 0), (0, LANES - N_EXPERTS)), constant_values=NEG_BIG).reshape(depth, 1, LANES)
    b_gu = b_gate_up.reshape(depth, N_EXPERTS, 1, 2 * D_FF)
    b_dn = b_down.reshape(depth, N_EXPERTS, 1, d)
    cos_t, sin_t = _rope_tables(dec_seq)
    cache_k = cache_diff_k.reshape(dec_batch, depth, past, DIFF_W)
    cache_v = cache_diff_v.reshape(dec_batch, depth, past, DIFF_W)

    mod_tab = _adaln(cond8, w_ada, b_ada).reshape(depth * 8 * 6, 1, d)
    x_all = jnp.concatenate([x_prompt.reshape(n_ctx, d), x_sample.reshape(n_lat, d)], axis=0)
    zero_state = jnp.zeros((batch, N_GLA_HEADS, GLA_DK, GLA_DV), F32)

    ks, vs, sts = [], [], []
    for l in range(depth):
        lam_init = 0.8 - 0.6 * math.exp(-0.3 * l)
        proj, la = _in_proj(x_all, norm_attn_w.reshape(depth, 1, d), mod_tab, w_main, w_ga, a2_pad, b_a,
                            l, n_ctx, dec_seq, tm_proj)
        qh, kt, vh, kc = _prep_ctx(proj, qw[l], kw[l], batch, seq)
        od_ctx = _attention(qh, kt, vh, diff_lambda_w[l], diff_out_norm_w[l].reshape(1, HEAD_W),
                            lam_init, ROW_TILE)
        og_ctx, s_f, s_b = _gla(proj, la, zero_state, zero_state, gla_out_norm_w[l].reshape(1, GLA_DV),
                                0, batch, seq, True)
        ks.append(kc.reshape(batch, seq, N_DIFF_HEADS, 2, DIFF_HEAD_DIM))
        vs.append(proj[:n_ctx, 2 * DIFF_W:3 * DIFF_W].reshape(batch, seq, N_DIFF_HEADS, HEAD_W))
        sts.append(jnp.stack([s_f, s_b], axis=1))
        qh, kt, vh = _prep_lat(proj, cache_k, cache_v, qw[l], kw[l], cos_t, sin_t, l, n_ctx,
                               dec_batch, dec_seq, past)
        od_lat = _attention(qh, kt, vh, diff_lambda_w[l], diff_out_norm_w[l].reshape(1, HEAD_W),
                            lam_init, ROW_TILE)
        og_lat, _, _ = _gla(proj, la, state_gla[:, l, 0], state_gla[:, l, 1],
                            gla_out_norm_w[l].reshape(1, GLA_DV), n_ctx, dec_batch, dec_seq, False)
        x_all, h2, top_idx, gates = _out_proj(x_all, od_ctx.reshape(n_ctx, DIFF_W), od_lat.reshape(n_lat, DIFF_W),
                                              og_ctx, og_lat, w_out_b, mod_tab,
                                              norm_ffn_w.reshape(depth, 1, d), rw_pad, rb_pad,
                                              l, n_ctx, dec_seq)
        items, sb_e, sb_start, sb_cnt, n_used = _route(top_idx[:, :TOP_K])
        y_items = _experts(h2, items, sb_e, sb_start, sb_cnt, n_used, w_gate_up, b_gu, w_down, b_dn, l)
        x_all = _combine(x_all, y_items, gates, mod_tab, l, n_ctx, dec_seq)

    y_prompt = x_all[:n_ctx].reshape(batch, seq, d)
    y_sample = x_all[n_ctx:].reshape(dec_batch, dec_seq, d)
    return (y_prompt, y_sample, jnp.stack(ks, axis=1), jnp.stack(vs, axis=1), jnp.stack(sts, axis=1))
```
